```python
import jax, jax.numpy as jnp
from jax import lax
import numpy as np

D_MODEL = 1024
BATCH = 2
SEQ = 8192
DEPTH = 2

N_MIXERS = 2
CHUNK = 128
PLE_DIM = 256
LN_EPS = 1e-5
ALPHA = (2 * DEPTH) ** 0.25
BETA = (8 * DEPTH) ** -0.25

GM_HALF = 3 * D_MODEL
GM_GROUPS = 8
GM_GROUP_DIM = GM_HALF // GM_GROUPS

ML_INNER = 2 * D_MODEL
ML_HEADS = 4
ML_HEAD_DIM = ML_INNER // ML_HEADS
ML_CONV = 4
ML_QKV_BLOCK = 4
ML_NBLOCKS = ML_INNER // ML_QKV_BLOCK

D_FF = -(-8 * D_MODEL // (3 * 256)) * 256

N_A = (DEPTH + 1) // 2
N_B = DEPTH // 2

kernel_name = "hybrid_gmlp_mlstm_deepnorm_trunk"


def layer_norm(x, g, b=None):
    xf = x.astype(jnp.float32)
    mu = jnp.mean(xf, axis=-1, keepdims=True)
    var = jnp.mean(jnp.square(xf - mu), axis=-1, keepdims=True)
    y = (xf - mu) * lax.rsqrt(var + LN_EPS) * g.astype(jnp.float32)
    if b is not None:
        y = y + b.astype(jnp.float32)
    return y.astype(x.dtype)


def gmlp_mixer(x, w_in, ln_g, ln_b, w_s, b_s, w_out):
    B, S, _ = x.shape
    nc = S // CHUNK
    z = jax.nn.gelu(x @ w_in)
    u, v = jnp.split(z, 2, axis=-1)
    v = layer_norm(v, ln_g, ln_b)
    v = v.reshape(B, nc, CHUNK, GM_GROUPS, GM_GROUP_DIM)
    causal = jnp.tril(jnp.ones((CHUNK, CHUNK), dtype=bool))
    ws = jnp.where(causal, w_s, jnp.zeros_like(w_s))
    sv = jnp.einsum('gts,bcsgd->bctgd', ws, v) + b_s.T[:, :, None]
    return (u * sv.reshape(B, S, GM_HALF)) @ w_out


def causal_depthwise_conv(x, w, b):
    K, C = w.shape
    y = lax.conv_general_dilated(x, w[:, None, :], window_strides=(1,), padding=[(K - 1, 0)],
                                 dimension_numbers=('NWC', 'WIO', 'NWC'), feature_group_count=C)
    return y + b


def headwise_linear(a, w):
    B, S, _ = a.shape
    a = a.reshape(B, S, ML_NBLOCKS, ML_QKV_BLOCK)
    return jnp.einsum('bsnc,ncd->bsnd', a, w).reshape(B, S, ML_INNER)


def mlstm_cell(q, k, v, i_pre, f_pre):
    out_dtype = v.dtype
    B, S, H, dh = q.shape
    nc = S // CHUNK
    f32 = jnp.float32

    def seq_chunks(a):
        return a.astype(f32).reshape(B, nc, CHUNK, H, dh).transpose(1, 0, 3, 2, 4)

    def gate_chunks(a):
        return a.astype(f32).reshape(B, nc, CHUNK, H).transpose(1, 0, 3, 2)

    qc = seq_chunks(q) * (dh ** -0.5)
    kc = seq_chunks(k)
    vc = seq_chunks(v)
    log_i = gate_chunks(i_pre)
    b_cum = jnp.cumsum(jax.nn.log_sigmoid(gate_chunks(f_pre)), axis=-1)
    causal = jnp.tril(jnp.ones((CHUNK, CHUNK), dtype=bool))

    def step(carry, inp):
        C, n, m = carry
        qb, kb, vb, bb, ib = inp
        dmat = bb[..., :, None] - bb[..., None, :] + ib[..., None, :]
        dmat = jnp.where(causal, dmat, -jnp.inf)
        m_inter = bb + m[..., None]
        m_t = jnp.maximum(m_inter, jnp.max(dmat, axis=-1))
        s = jnp.einsum('bhtd,bhsd->bhts', qb, kb) * jnp.exp(dmat - m_t[..., None])
        scale_inter = jnp.exp(m_inter - m_t)
        num = (jnp.einsum('bhts,bhsd->bhtd', s, vb)
               + scale_inter[..., None] * jnp.einsum('bhtk,bhkv->bhtv', qb, C))
        den = jnp.sum(s, axis=-1) + scale_inter * jnp.einsum('bhtk,bhk->bht', qb, n)
        h = num / jnp.maximum(jnp.abs(den), jnp.exp(-m_t))[..., None]
        b_last = bb[..., -1]
        g = b_last[..., None] - bb + ib
        m_new = jnp.maximum(b_last + m, jnp.max(g, axis=-1))
        wg = jnp.exp(g - m_new[..., None])
        decay = jnp.exp(b_last + m - m_new)
        C = decay[..., None, None] * C + jnp.einsum('bhs,bhsk,bhsv->bhkv', wg, kb, vb)
        n = decay[..., None] * n + jnp.einsum('bhs,bhsk->bhk', wg, kb)
        return (C, n, m_new), h

    init = (jnp.zeros((B, H, dh, dh), f32), jnp.zeros((B, H, dh), f32), jnp.zeros((B, H), f32))
    _, h = lax.scan(step, init, (qc, kc, vc, b_cum, log_i))
    return h.transpose(1, 0, 3, 2, 4).reshape(B, S, H, dh).astype(out_dtype)


def mlstm_mixer(x, w_in, conv_w, conv_b, w_q, w_k, w_v, w_gates, b_gates, skip, norm_g, w_out):
    B, S, _ = x.shape
    xm, z = jnp.split(x @ w_in, 2, axis=-1)
    xc = jax.nn.silu(causal_depthwise_conv(xm, conv_w, conv_b))
    q = headwise_linear(xc, w_q)
    k = headwise_linear(xc, w_k)
    v = headwise_linear(xm, w_v)
    gates = jnp.concatenate([q, k, v], axis=-1) @ w_gates + b_gates
    i_pre, f_pre = gates[..., :ML_HEADS], gates[..., ML_HEADS:]
    split_heads = lambda a: a.reshape(B, S, ML_HEADS, ML_HEAD_DIM)
    h = mlstm_cell(split_heads(q), split_heads(k), split_heads(v), i_pre, f_pre)
    h = layer_norm(h, norm_g).reshape(B, S, ML_INNER) + skip * xc
    return (h * jax.nn.silu(z)) @ w_out


def swiglu(x, w_gate, w_up, w_down):
    return (jax.nn.silu(x @ w_gate) * (x @ w_up)) @ w_down


def setup_inputs(seed: int = 0) -> dict:
    key = jax.random.key(seed)
    ks = iter(jax.random.split(key, 40))
    nrm = lambda shape, scale: jax.random.normal(next(ks), shape, jnp.float32) * scale
    ones_ish = lambda shape: 1.0 + nrm(shape, 0.02)
    b_gates_one = jnp.concatenate([jnp.zeros((ML_HEADS,), jnp.float32),
                                   jnp.linspace(3.0, 6.0, ML_HEADS, dtype=jnp.float32)])
    return {
        "x": nrm((BATCH, SEQ, D_MODEL), 1.0),
        "p": nrm((DEPTH, BATCH, SEQ, PLE_DIM), 1.0),
        "gm_w_in": nrm((N_A, D_MODEL, 2 * GM_HALF), D_MODEL ** -0.5),
        "gm_ln_g": ones_ish((N_A, GM_HALF)),
        "gm_ln_b": nrm((N_A, GM_HALF), 0.02),
        "gm_w_s": nrm((N_A, GM_GROUPS, CHUNK, CHUNK), CHUNK ** -0.5),
        "gm_b_s": ones_ish((N_A, GM_GROUPS, CHUNK)),
        "gm_w_out": nrm((N_A, GM_HALF, D_MODEL), BETA * GM_HALF ** -0.5),
        "ml_w_in": nrm((N_B, D_MODEL, 2 * ML_INNER), D_MODEL ** -0.5),
        "ml_conv_w": nrm((N_B, ML_CONV, ML_INNER), ML_CONV ** -0.5),
        "ml_conv_b": nrm((N_B, ML_INNER), 0.02),
        "ml_w_q": nrm((N_B, ML_NBLOCKS, ML_QKV_BLOCK, ML_QKV_BLOCK), ML_QKV_BLOCK ** -0.5),
        "ml_w_k": nrm((N_B, ML_NBLOCKS, ML_QKV_BLOCK, ML_QKV_BLOCK), ML_QKV_BLOCK ** -0.5),
        "ml_w_v": nrm((N_B, ML_NBLOCKS, ML_QKV_BLOCK, ML_QKV_BLOCK), ML_QKV_BLOCK ** -0.5),
        "ml_w_gates": nrm((N_B, 3 * ML_INNER, 2 * ML_HEADS), (3 * ML_INNER) ** -0.5),
        "ml_b_gates": b_gates_one[None] + nrm((N_B, 2 * ML_HEADS), 0.1),
        "ml_skip": ones_ish((N_B, ML_INNER)),
        "ml_norm_g": ones_ish((N_B, ML_HEADS, ML_HEAD_DIM)),
        "ml_w_out": nrm((N_B, ML_INNER, D_MODEL), BETA * ML_INNER ** -0.5),
        "ln1_g": ones_ish((DEPTH, D_MODEL)),
        "ln1_b": nrm((DEPTH, D_MODEL), 0.02),
        "ln2_g": ones_ish((DEPTH, D_MODEL)),
        "ln2_b": nrm((DEPTH, D_MODEL), 0.02),
        "ffn_w_gate": nrm((DEPTH, D_MODEL, D_FF), D_MODEL ** -0.5),
        "ffn_w_up": nrm((DEPTH, D_MODEL, D_FF), D_MODEL ** -0.5),
        "ffn_w_down": nrm((DEPTH, D_FF, D_MODEL), BETA * D_FF ** -0.5),
        "ple_w_proj": nrm((DEPTH, PLE_DIM, D_MODEL), PLE_DIM ** -0.5),
        "ple_w_gate": nrm((DEPTH, D_MODEL, D_MODEL), D_MODEL ** -0.5),
        "ple_b_gate": nrm((DEPTH, D_MODEL), 0.02),
    }


def reference(x, p, gm_w_in, gm_ln_g, gm_ln_b, gm_w_s, gm_b_s, gm_w_out,
              ml_w_in, ml_conv_w, ml_conv_b, ml_w_q, ml_w_k, ml_w_v, ml_w_gates, ml_b_gates,
              ml_skip, ml_norm_g, ml_w_out, ln1_g, ln1_b, ln2_g, ln2_b,
              ffn_w_gate, ffn_w_up, ffn_w_down, ple_w_proj, ple_w_gate, ple_b_gate):
    for i in range(DEPTH):
        j = i // N_MIXERS
        if i % N_MIXERS == 0:
            y = gmlp_mixer(x, gm_w_in[j], gm_ln_g[j], gm_ln_b[j], gm_w_s[j], gm_b_s[j], gm_w_out[j])
        else:
            y = mlstm_mixer(x, ml_w_in[j], ml_conv_w[j], ml_conv_b[j], ml_w_q[j], ml_w_k[j],
                            ml_w_v[j], ml_w_gates[j], ml_b_gates[j], ml_skip[j], ml_norm_g[j],
                            ml_w_out[j])
        x = layer_norm(ALPHA * x + y, ln1_g[i], ln1_b[i])
        x = layer_norm(ALPHA * x + swiglu(x, ffn_w_gate[i], ffn_w_up[i], ffn_w_down[i]),
                       ln2_g[i], ln2_b[i])
        gate = jax.nn.sigmoid(x @ ple_w_gate[i] + ple_b_gate[i])
        x = x + gate * (p[i] @ ple_w_proj[i])
    return x
```

```python
import functools

import jax
import jax.numpy as jnp
from jax import lax
from jax.experimental import pallas as pl
from jax.experimental.pallas import tpu as pltpu

F32 = jnp.float32
BF16 = jnp.bfloat16

CHUNK = 128
LN_EPS = 1e-5
DEPTH = 2
ALPHA = (2 * DEPTH) ** 0.25
GM_GROUPS = 8
ML_HEADS = 4
ML_CONV = 4
ML_QKV_BLOCK = 4

LANES = 128
MXU_TILE = 256
VMEM_LIMIT_BYTES = 56 * 1024 * 1024


def _dot(a, b):
    return jnp.dot(a, b, preferred_element_type=F32)


def _sigmoid(x):
    return 1.0 / (1.0 + jnp.exp(-x))


def _gelu_tanh(x):
    c = 0.7978845608028654
    return 0.5 * x * (1.0 + jnp.tanh(c * (x + 0.044715 * (x * x * x))))


def _layer_norm(x, g, b=None):
    mu = jnp.mean(x, axis=-1, keepdims=True)
    xc = x - mu
    var = jnp.mean(xc * xc, axis=-1, keepdims=True)
    y = xc * lax.rsqrt(var + LN_EPS) * g
    if b is not None:
        y = y + b
    return y


def _resident(shape):
    nd = len(shape)
    return pl.BlockSpec(shape, lambda *_: (0,) * nd, pipeline_mode=pl.Buffered(1))


def _params(semantics):
    return pltpu.CompilerParams(dimension_semantics=semantics,
                                vmem_limit_bytes=VMEM_LIMIT_BYTES)


def _gmlp_kernel(x_ref, wu_ref, wv_ref, lng_ref, lnb_ref, ws_ref, bs_ref, wo_ref,
                 g1_ref, b1_ref, o_ref, vn_ref, y_ref, *, tb, half):
    gdim = half // GM_GROUPS
    x = x_ref[...]
    xb = x.astype(BF16)
    v = _gelu_tanh(_dot(xb, wv_ref[...]))
    vn_ref[...] = _layer_norm(v, lng_ref[...], lnb_ref[...]).astype(BF16)

    row = lax.broadcasted_iota(jnp.int32, (CHUNK, CHUNK), 0)
    col = lax.broadcasted_iota(jnp.int32, (CHUNK, CHUNK), 1)
    causal = col <= row
    bs = bs_ref[...]
    pair = 2 * gdim
    for j in range(GM_GROUPS // 2):
        u2 = _gelu_tanh(_dot(xb, wu_ref[:, j * pair:(j + 1) * pair]))
        for gg in range(2):
            g = 2 * j + gg
            wsg = jnp.where(causal, ws_ref[g], 0.0).astype(BF16)
            bg = bs[:, g:g + 1]
            for c in range(tb // CHUNK):
                rows = slice(c * CHUNK, (c + 1) * CHUNK)
                sv = _dot(wsg, vn_ref[rows, g * gdim:(g + 1) * gdim]) + bg
                y_ref[rows, g * gdim:(g + 1) * gdim] = (
                    u2[rows, gg * gdim:(gg + 1) * gdim] * sv).astype(BF16)
    y = _dot(y_ref[...], wo_ref[...])
    o_ref[...] = _layer_norm(ALPHA * x + y, g1_ref[...], b1_ref[...])


def _gmlp_layer(x2d, w_in, ln_g, ln_b, w_s, b_s, w_out, g1, b1, *, tb):
    t, d = x2d.shape
    half = w_in.shape[1] // 2
    assert t % tb == 0 and tb % CHUNK == 0 and half % (GM_GROUPS * LANES) == 0
    wu = w_in[:, :half].astype(BF16)
    wv = w_in[:, half:].astype(BF16)
    row = lambda a: a.reshape(1, -1)
    tok = pl.BlockSpec((tb, d), lambda i: (i, 0))
    return pl.pallas_call(
        functools.partial(_gmlp_kernel, tb=tb, half=half),
        out_shape=jax.ShapeDtypeStruct((t, d), F32),
        grid=(t // tb,),
        in_specs=[tok, _resident((d, half)), _resident((d, half)),
                  _resident((1, half)), _resident((1, half)),
                  _resident(w_s.shape), _resident((CHUNK, GM_GROUPS)),
                  _resident((half, d)), _resident((1, d)), _resident((1, d))],
        out_specs=tok,
        scratch_shapes=[pltpu.VMEM((tb, half), BF16), pltpu.VMEM((tb, half), BF16)],
        compiler_params=_params(("arbitrary",)),
        name="gmlp_layer",
    )(x2d, wu, wv, row(ln_g), row(ln_b), w_s, b_s.T, w_out.astype(BF16), row(g1), row(b1))


def _ffn_kernel(x_ref, p_ref, wg_ref, wu_ref, wd_ref, g2_ref, b2_ref,
                wpg_ref, bpg_ref, wpp_ref, o_ref):
    x = x_ref[...]
    xb = x.astype(BF16)
    g = _dot(xb, wg_ref[...])
    u = _dot(xb, wu_ref[...])
    h = (g * _sigmoid(g) * u).astype(BF16)
    y = _dot(h, wd_ref[...])
    x2 = _layer_norm(ALPHA * x + y, g2_ref[...], b2_ref[...])
    gate = _sigmoid(_dot(x2.astype(BF16), wpg_ref[...]) + bpg_ref[...])
    pp = _dot(p_ref[...].astype(BF16), wpp_ref[...])
    o_ref[...] = x2 + gate * pp


def _ffn_layer(x2d, p2d, w_gate, w_up, w_down, g2, b2, wpg, bpg, wpp, *, tb):
    t, d = x2d.shape
    dff = w_gate.shape[1]
    dp = p2d.shape[1]
    assert t % tb == 0
    row = lambda a: a.reshape(1, -1)
    tok = pl.BlockSpec((tb, d), lambda i: (i, 0))
    return pl.pallas_call(
        _ffn_kernel,
        out_shape=jax.ShapeDtypeStruct((t, d), F32),
        grid=(t // tb,),
        in_specs=[tok, pl.BlockSpec((tb, dp), lambda i: (i, 0)),
                  _resident((d, dff)), _resident((d, dff)), _resident((dff, d)),
                  _resident((1, d)), _resident((1, d)),
                  _resident((d, d)), _resident((1, d)), _resident((dp, d))],
        out_specs=tok,
        compiler_params=_params(("arbitrary",)),
        name="ffn_layer",
    )(x2d, p2d, w_gate.astype(BF16), w_up.astype(BF16), w_down.astype(BF16),
      row(g2), row(b2), wpg.astype(BF16), row(bpg), wpp.astype(BF16))


HALO = 8


def _mlstm_proj_kernel(x_ref, wm_ref, wz_ref, cw_ref, cb_ref, wqk_ref, wv_ref,
                       wgq_ref, wgk_ref, wgv_ref, bg_ref,
                       q_ref, k_ref, v_ref, xc_ref, z_ref, gt_ref, ext_ref, *, tb, inner):
    xb = x_ref[...].astype(BF16)
    z_ref[...] = _dot(xb, wz_ref[...]).astype(BF16)
    xm = _dot(xb, wm_ref[...])

    @pl.when(pl.program_id(1) == 0)
    def _():
        ext_ref[0:HALO, :] = jnp.zeros((HALO, inner), F32)

    ext_ref[HALO:HALO + tb, :] = xm
    cw = cw_ref[...]
    conv = cw[ML_CONV - 1:ML_CONV, :] * xm + cb_ref[...]
    for j in range(1, ML_CONV):
        conv = conv + cw[ML_CONV - 1 - j:ML_CONV - j, :] * ext_ref[HALO - j:HALO - j + tb, :]
    ext_ref[0:HALO, :] = ext_ref[tb:tb + HALO, :]
    xc = conv * _sigmoid(conv)
    xcb = xc.astype(BF16)
    xmb = xm.astype(BF16)
    xc_ref[...] = xcb

    gates = jnp.broadcast_to(bg_ref[...], (tb, LANES))
    for g in range(inner // MXU_TILE):
        sl = slice(g * MXU_TILE, (g + 1) * MXU_TILE)
        qk = _dot(xcb[:, sl], wqk_ref[g])
        qg = qk[:, :MXU_TILE].astype(BF16)
        kg = qk[:, MXU_TILE:].astype(BF16)
        vg = _dot(xmb[:, sl], wv_ref[g]).astype(BF16)
        q_ref[:, sl] = qg
        k_ref[:, sl] = kg
        v_ref[:, sl] = vg
        gates = (gates + _dot(qg, wgq_ref[sl, :]) + _dot(kg, wgk_ref[sl, :])
                 + _dot(vg, wgv_ref[sl, :]))
    gt_ref[...] = gates


def _block_diag_tiles(w, tile):
    nb, c, d = w.shape
    per = tile // c
    w = w.reshape(nb // per, per, c, d)
    eye = jnp.eye(per, dtype=w.dtype)
    out = w[:, :, :, None, :] * eye[None, :, None, :, None]
    return out.reshape(nb // per, tile, tile)


def _mlstm_proj(x2d, w_in, conv_w, conv_b, w_q, w_k, w_v, w_gates, b_gates, *, batch, tb):
    t, d = x2d.shape
    inner = w_in.shape[1] // 2
    seq = t // batch
    assert seq % tb == 0 and inner % MXU_TILE == 0 and tb >= HALO
    nblk = seq // tb
    ntile = inner // MXU_TILE
    wqk = jnp.concatenate([_block_diag_tiles(w_q, MXU_TILE), _block_diag_tiles(w_k, MXU_TILE)],
                          axis=-1).astype(BF16)
    wv = _block_diag_tiles(w_v, MXU_TILE).astype(BF16)
    ng = w_gates.shape[1]
    wg = jnp.pad(w_gates, ((0, 0), (0, LANES - ng))).astype(BF16)
    bg = jnp.pad(b_gates, (0, LANES - ng)).reshape(1, LANES)
    tok = lambda w: pl.BlockSpec((tb, w), lambda b, j: (b * nblk + j, 0))
    act = jax.ShapeDtypeStruct((t, inner), BF16)
    return pl.pallas_call(
        functools.partial(_mlstm_proj_kernel, tb=tb, inner=inner),
        out_shape=(act, act, act, act, act, jax.ShapeDtypeStruct((t, LANES), F32)),
        grid=(batch, nblk),
        in_specs=[tok(d), _resident((d, inner)), _resident((d, inner)),
                  _resident((ML_CONV, inner)), _resident((1, inner)),
                  _resident((ntile, MXU_TILE, 2 * MXU_TILE)), _resident((ntile, MXU_TILE, MXU_TILE)),
                  _resident((inner, LANES)), _resident((inner, LANES)), _resident((inner, LANES)),
                  _resident((1, LANES))],
        out_specs=(tok(inner), tok(inner), tok(inner), tok(inner), tok(inner), tok(LANES)),
        scratch_shapes=[pltpu.VMEM((tb + HALO, inner), F32)],
        compiler_params=_params(("arbitrary", "arbitrary")),
        name="mlstm_proj",
    )(x2d, w_in[:, :inner].astype(BF16), w_in[:, inner:].astype(BF16), conv_w,
      conv_b.reshape(1, inner), wqk, wv, wg[:inner], wg[inner:2 * inner], wg[2 * inner:], bg)


def _split3(a):
    hi = a.astype(BF16)
    r1 = a - hi.astype(F32)
    mid = r1.astype(BF16)
    lo = (r1 - mid.astype(F32)).astype(BF16)
    return hi, mid, lo


def _mlstm_cell_kernel(q_ref, k_ref, v_ref, xc_ref, z_ref, gt_ref, x_ref,
                       skip_ref, ng_ref, wo_ref, g1_ref, b1_ref, o_ref,
                       c_ref, n_ref, m_ref, h_ref, *, tb, dh):
    nh = ML_HEADS
    scale = dh ** -0.5

    @pl.when(pl.program_id(1) == 0)
    def _():
        c_ref[...] = jnp.zeros(c_ref.shape, F32)
        n_ref[...] = jnp.zeros(n_ref.shape, F32)
        m_ref[...] = jnp.zeros(m_ref.shape, F32)

    row = lax.broadcasted_iota(jnp.int32, (CHUNK, CHUNK), 0)
    col = lax.broadcasted_iota(jnp.int32, (CHUNK, CHUNK), 1)
    causal = col <= row
    tri = jnp.where(causal, 1.0, 0.0).astype(BF16)
    lane = lax.broadcasted_iota(jnp.int32, (CHUNK, LANES), 1)

    for c in range(tb // CHUNK):
        rows = slice(c * CHUNK, (c + 1) * CHUNK)
        gt = gt_ref[rows, :]
        lf = jnp.minimum(gt, 0.0) - jnp.log(1.0 + jnp.exp(-jnp.abs(gt)))
        hi, mid, lo = _split3(lf)
        bcum = _dot(tri, hi) + _dot(tri, mid) + _dot(tri, lo)
        colmat = jnp.where(lane < nh, gt, bcum)
        rowmat = colmat.T
        for h in range(nh):
            hs = slice(h * dh, (h + 1) * dh)
            i_row = rowmat[h:h + 1, :]
            b_row = rowmat[nh + h:nh + h + 1, :]
            i_col = colmat[:, h:h + 1]
            b_col = colmat[:, nh + h:nh + h + 1]
            m_prev = m_ref[h:h + 1, 0:1]
            qh = q_ref[rows, hs]
            kh = k_ref[rows, hs]
            vh = v_ref[rows, hs]

            dmat = jnp.where(causal, b_col - b_row + i_row, -jnp.inf)
            m_inter = b_col + m_prev
            m_t = jnp.maximum(m_inter, jnp.max(dmat, axis=-1, keepdims=True))
            qk = lax.dot_general(qh, kh, (((1,), (1,)), ((), ())), preferred_element_type=F32)
            s = qk * scale * jnp.exp(dmat - m_t)
            si = jnp.exp(m_inter - m_t) * scale
            num = _dot(s.astype(BF16), vh) + si * _dot(qh, c_ref[h].astype(BF16))
            qn = jnp.sum(qh.astype(F32) * n_ref[h], axis=-1, keepdims=True)
            den = jnp.sum(s, axis=-1, keepdims=True) + si * qn
            h_ref[rows, hs] = num / jnp.maximum(jnp.abs(den), jnp.exp(-m_t))

            b_last = b_row[:, CHUNK - 1:CHUNK]
            m_new = jnp.maximum(b_last + m_prev,
                                jnp.max(b_last - b_row + i_row, axis=-1, keepdims=True))
            wg_col = jnp.exp(b_last - b_col + i_col - m_new)
            decay = jnp.exp(b_last + m_prev - m_new)
            kw = kh.astype(F32) * wg_col
            c_ref[h] = decay * c_ref[h] + lax.dot_general(
                kw.astype(BF16), vh, (((0,), (0,)), ((), ())), preferred_element_type=F32)
            n_ref[h] = decay * n_ref[h] + jnp.sum(kw, axis=0, keepdims=True)
            m_ref[h:h + 1, :] = jnp.broadcast_to(m_new, (1, LANES))

    ng = ng_ref[...]
    for h in range(nh):
        hs = slice(h * dh, (h + 1) * dh)
        h_ref[:, hs] = _layer_norm(h_ref[:, hs], ng[:, hs])
    z = z_ref[...].astype(F32)
    hg = (h_ref[...] + skip_ref[...] * xc_ref[...].astype(F32)) * (z * _sigmoid(z))
    y = _dot(hg.astype(BF16), wo_ref[...])
    o_ref[...] = _layer_norm(ALPHA * x_ref[...] + y, g1_ref[...], b1_ref[...])


def _mlstm_cell_layer(q, k, v, xc, z, gt, x2d, skip, norm_g, w_out, g1, b1, *, batch, tb):
    t, inner = q.shape
    d = x2d.shape[1]
    dh = inner // ML_HEADS
    seq = t // batch
    assert seq % tb == 0 and tb % CHUNK == 0
    nblk = seq // tb
    row = lambda a: a.reshape(1, -1)
    tok = lambda w: pl.BlockSpec((tb, w), lambda b, j: (b * nblk + j, 0))
    return pl.pallas_call(
        functools.partial(_mlstm_cell_kernel, tb=tb, dh=dh),
        out_shape=jax.ShapeDtypeStruct((t, d), F32),
        grid=(batch, nblk),
        in_specs=[tok(inner), tok(inner), tok(inner), tok(inner), tok(inner), tok(LANES), tok(d),
                  _resident((1, inner)), _resident((1, inner)), _resident((inner, d)),
                  _resident((1, d)), _resident((1, d))],
        out_specs=tok(d),
        scratch_shapes=[pltpu.VMEM((ML_HEADS, dh, dh), F32),
                        pltpu.VMEM((ML_HEADS, 1, dh), F32),
                        pltpu.VMEM((8, LANES), F32),
                        pltpu.VMEM((tb, inner), F32)],
        compiler_params=_params(("arbitrary", "arbitrary")),
        name="mlstm_cell",
    )(q, k, v, xc, z, gt, x2d, row(skip), row(norm_g), w_out.astype(BF16), row(g1), row(b1))


def kernel(x, p, gm_w_in, gm_ln_g, gm_ln_b, gm_w_s, gm_b_s, gm_w_out, ml_w_in, ml_conv_w, ml_conv_b, ml_w_q, ml_w_k, ml_w_v, ml_w_gates, ml_b_gates, ml_skip, ml_norm_g, ml_w_out, ln1_g, ln1_b, ln2_g, ln2_b, ffn_w_gate, ffn_w_up, ffn_w_down, ple_w_proj, ple_w_gate, ple_b_gate):
    batch, seq, d = x.shape
    t = batch * seq
    xs = x.reshape(t, d)
    ps = p.reshape(p.shape[0], t, p.shape[-1])

    def ffn(xs, i):
        return _ffn_layer(xs, ps[i], ffn_w_gate[i], ffn_w_up[i], ffn_w_down[i],
                          ln2_g[i], ln2_b[i], ple_w_gate[i], ple_b_gate[i], ple_w_proj[i], tb=512)

    xs = _gmlp_layer(xs, gm_w_in[0], gm_ln_g[0], gm_ln_b[0], gm_w_s[0], gm_b_s[0], gm_w_out[0],
                     ln1_g[0], ln1_b[0], tb=512)
    xs = ffn(xs, 0)
    q, k, v, xc, z, gt = _mlstm_proj(xs, ml_w_in[0], ml_conv_w[0], ml_conv_b[0], ml_w_q[0],
                                     ml_w_k[0], ml_w_v[0], ml_w_gates[0], ml_b_gates[0],
                                     batch=batch, tb=512)
    xs = _mlstm_cell_layer(q, k, v, xc, z, gt, xs, ml_skip[0], ml_norm_g[0], ml_w_out[0],
                           ln1_g[1], ln1_b[1], batch=batch, tb=256)
    xs = ffn(xs, 1)
    return xs.reshape(batch, seq, d)
```

```python
import functools

import jax
import jax.numpy as jnp
from jax import lax
from jax.experimental import pallas as pl
from jax.experimental.pallas import tpu as pltpu

F32 = jnp.float32
BF16 = jnp.bfloat16

CHUNK = 128
LN_EPS = 1e-5
DEPTH = 2
ALPHA = (2 * DEPTH) ** 0.25
GM_GROUPS = 8
ML_HEADS = 4
ML_CONV = 4
ML_QKV_BLOCK = 4

LANES = 128
MXU_TILE = 256
VMEM_LIMIT_BYTES = 56 * 1024 * 1024


def _dot(a, b):
    return jnp.dot(a, b, preferred_element_type=F32)


def _sigmoid(x):
    return 1.0 / (1.0 + jnp.exp(-x))


def _gelu_tanh(x):
    c = 0.7978845608028654
    return 0.5 * x * (1.0 + jnp.tanh(c * (x + 0.044715 * (x * x * x))))


def _layer_norm(x, g, b=None):
    mu = jnp.mean(x, axis=-1, keepdims=True)
    xc = x - mu
    var = jnp.mean(xc * xc, axis=-1, keepdims=True)
    y = xc * lax.rsqrt(var + LN_EPS) * g
    if b is not None:
        y = y + b
    return y


def _resident(shape):
    nd = len(shape)
    return pl.BlockSpec(shape, lambda *_: (0,) * nd, pipeline_mode=pl.Buffered(1))


def _params(semantics):
    return pltpu.CompilerParams(dimension_semantics=semantics,
                                vmem_limit_bytes=VMEM_LIMIT_BYTES)


def _gmlp_kernel(x_ref, wu_ref, wv_ref, lng_ref, lnb_ref, ws_ref, bs_ref, wo_ref,
                 g1_ref, b1_ref, o_ref, vn_ref, y_ref, *, tb, half):
    gdim = half // GM_GROUPS
    x = x_ref[...]
    xb = x.astype(BF16)
    v = _gelu_tanh(_dot(xb, wv_ref[...]))
    vn_ref[...] = _layer_norm(v, lng_ref[...], lnb_ref[...]).astype(BF16)

    row = lax.broadcasted_iota(jnp.int32, (CHUNK, CHUNK), 0)
    col = lax.broadcasted_iota(jnp.int32, (CHUNK, CHUNK), 1)
    causal = col <= row
    bs = bs_ref[...]
    pair = 2 * gdim
    for j in range(GM_GROUPS // 2):
        u2 = _gelu_tanh(_dot(xb, wu_ref[:, j * pair:(j + 1) * pair]))
        for gg in range(2):
            g = 2 * j + gg
            wsg = jnp.where(causal, ws_ref[g], 0.0).astype(BF16)
            bg = bs[:, g:g + 1]
            for c in range(tb // CHUNK):
                rows = slice(c * CHUNK, (c + 1) * CHUNK)
                sv = _dot(wsg, vn_ref[rows, g * gdim:(g + 1) * gdim]) + bg
                y_ref[rows, g * gdim:(g + 1) * gdim] = (
                    u2[rows, gg * gdim:(gg + 1) * gdim] * sv).astype(BF16)
    y = _dot(y_ref[...], wo_ref[...])
    o_ref[...] = _layer_norm(ALPHA * x + y, g1_ref[...], b1_ref[...])


def _gmlp_layer(x2d, w_in, ln_g, ln_b, w_s, b_s, w_out, g1, b1, *, tb):
    t, d = x2d.shape
    half = w_in.shape[1] // 2
    assert t % tb == 0 and tb % CHUNK == 0 and half % (GM_GROUPS * LANES) == 0
    wu = w_in[:, :half].astype(BF16)
    wv = w_in[:, half:].astype(BF16)
    row = lambda a: a.reshape(1, -1)
    tok = pl.BlockSpec((tb, d), lambda i: (i, 0))
    return pl.pallas_call(
        functools.partial(_gmlp_kernel, tb=tb, half=half),
        out_shape=jax.ShapeDtypeStruct((t, d), F32),
        grid=(t // tb,),
        in_specs=[tok, _resident((d, half)), _resident((d, half)),
                  _resident((1, half)), _resident((1, half)),
                  _resident(w_s.shape), _resident((CHUNK, GM_GROUPS)),
                  _resident((half, d)), _resident((1, d)), _resident((1, d))],
        out_specs=tok,
        scratch_shapes=[pltpu.VMEM((tb, half), BF16), pltpu.VMEM((tb, half), BF16)],
        compiler_params=_params(("arbitrary",)),
        name="gmlp_layer",
    )(x2d, wu, wv, row(ln_g), row(ln_b), w_s, b_s.T, w_out.astype(BF16), row(g1), row(b1))


def _ffn_kernel(x_ref, p_ref, wg_ref, wu_ref, wd_ref, g2_ref, b2_ref,
                wpg_ref, bpg_ref, wpp_ref, o_ref):
    x = x_ref[...]
    xb = x.astype(BF16)
    g = _dot(xb, wg_ref[...])
    u = _dot(xb, wu_ref[...])
    h = (g * _sigmoid(g) * u).astype(BF16)
    y = _dot(h, wd_ref[...])
    x2 = _layer_norm(ALPHA * x + y, g2_ref[...], b2_ref[...])
    gate = _sigmoid(_dot(x2.astype(BF16), wpg_ref[...]) + bpg_ref[...])
    pp = _dot(p_ref[...].astype(BF16), wpp_ref[...])
    o_ref[...] = x2 + gate * pp


def _ffn_layer(x2d, p2d, w_gate, w_up, w_down, g2, b2, wpg, bpg, wpp, *, tb):
    t, d = x2d.shape
    dff = w_gate.shape[1]
    dp = p2d.shape[1]
    assert t % tb == 0
    row = lambda a: a.reshape(1, -1)
    tok = pl.BlockSpec((tb, d), lambda i: (i, 0))
    return pl.pallas_call(
        _ffn_kernel,
        out_shape=jax.ShapeDtypeStruct((t, d), F32),
        grid=(t // tb,),
        in_specs=[tok, pl.BlockSpec((tb, dp), lambda i: (i, 0)),
                  _resident((d, dff)), _resident((d, dff)), _resident((dff, d)),
                  _resident((1, d)), _resident((1, d)),
                  _resident((d, d)), _resident((1, d)), _resident((dp, d))],
        out_specs=tok,
        compiler_params=_params(("arbitrary",)),
        name="ffn_layer",
    )(x2d, p2d, w_gate.astype(BF16), w_up.astype(BF16), w_down.astype(BF16),
      row(g2), row(b2), wpg.astype(BF16), row(bpg), wpp.astype(BF16))


HALO = 8


def _dot_nt(a, b):
    return lax.dot_general(a, b, (((1,), (1,)), ((), ())), preferred_element_type=F32)


def _mlstm_proj_kernel(x_ref, wm_ref, wz_ref, cw_ref, cb_ref, wqk_ref, wkt_ref, wv_ref,
                       wgq_ref, wgk_ref, wgv_ref, bg_ref,
                       q_ref, kt_ref, v_ref, xc_ref, z_ref, gt_ref, ext_ref, *, tb, inner):
    xb = x_ref[...].astype(BF16)
    z_ref[...] = _dot(xb, wz_ref[...]).astype(BF16)
    xm = _dot(xb, wm_ref[...])

    @pl.when(pl.program_id(1) == 0)
    def _():
        ext_ref[0:HALO, :] = jnp.zeros((HALO, inner), F32)

    ext_ref[HALO:HALO + tb, :] = xm
    cw = cw_ref[...]
    conv = cw[ML_CONV - 1:ML_CONV, :] * xm + cb_ref[...]
    for j in range(1, ML_CONV):
        conv = conv + cw[ML_CONV - 1 - j:ML_CONV - j, :] * ext_ref[HALO - j:HALO - j + tb, :]
    ext_ref[0:HALO, :] = ext_ref[tb:tb + HALO, :]
    xc = conv * _sigmoid(conv)
    xcb = xc.astype(BF16)
    xmb = xm.astype(BF16)
    xc_ref[...] = xcb

    gates = jnp.broadcast_to(bg_ref[...], (tb, LANES))
    for g in range(inner // MXU_TILE):
        sl = slice(g * MXU_TILE, (g + 1) * MXU_TILE)
        qk = _dot(xcb[:, sl], wqk_ref[g])
        qg = qk[:, :MXU_TILE].astype(BF16)
        kg = qk[:, MXU_TILE:].astype(BF16)
        vg = _dot(xmb[:, sl], wv_ref[g]).astype(BF16)
        q_ref[:, sl] = qg
        kt_ref[sl, :] = _dot_nt(wkt_ref[g], xcb[:, sl]).astype(BF16)
        v_ref[:, sl] = vg
        gates = (gates + _dot(qg, wgq_ref[sl, :]) + _dot(kg, wgk_ref[sl, :])
                 + _dot(vg, wgv_ref[sl, :]))
    gt_ref[...] = gates


def _block_diag_tiles(w, tile):
    nb, c, d = w.shape
    per = tile // c
    w = w.reshape(nb // per, per, c, d)
    eye = jnp.eye(per, dtype=w.dtype)
    out = w[:, :, :, None, :] * eye[None, :, None, :, None]
    return out.reshape(nb // per, tile, tile)


def _mlstm_proj(x2d, w_in, conv_w, conv_b, w_q, w_k, w_v, w_gates, b_gates, *, batch, tb):
    t, d = x2d.shape
    inner = w_in.shape[1] // 2
    seq = t // batch
    assert seq % tb == 0 and inner % MXU_TILE == 0 and tb >= HALO
    nblk = seq // tb
    ntile = inner // MXU_TILE
    wk_bd = _block_diag_tiles(w_k, MXU_TILE)
    wqk = jnp.concatenate([_block_diag_tiles(w_q, MXU_TILE), wk_bd], axis=-1).astype(BF16)
    wkt = jnp.swapaxes(wk_bd, 1, 2).astype(BF16)
    wv = _block_diag_tiles(w_v, MXU_TILE).astype(BF16)
    ng = w_gates.shape[1]
    wg = jnp.pad(w_gates, ((0, 0), (0, LANES - ng))).astype(BF16)
    bg = jnp.pad(b_gates, (0, LANES - ng)).reshape(1, LANES)
    tok = lambda w: pl.BlockSpec((tb, w), lambda b, j: (b * nblk + j, 0))
    tok_t = pl.BlockSpec((inner, tb), lambda b, j: (0, b * nblk + j))
    act = jax.ShapeDtypeStruct((t, inner), BF16)
    act_t = jax.ShapeDtypeStruct((inner, t), BF16)
    return pl.pallas_call(
        functools.partial(_mlstm_proj_kernel, tb=tb, inner=inner),
        out_shape=(act, act_t, act, act, act, jax.ShapeDtypeStruct((t, LANES), F32)),
        grid=(batch, nblk),
        in_specs=[tok(d), _resident((d, inner)), _resident((d, inner)),
                  _resident((ML_CONV, inner)), _resident((1, inner)),
                  _resident((ntile, MXU_TILE, 2 * MXU_TILE)), _resident((ntile, MXU_TILE, MXU_TILE)),
                  _resident((ntile, MXU_TILE, MXU_TILE)),
                  _resident((inner, LANES)), _resident((inner, LANES)), _resident((inner, LANES)),
                  _resident((1, LANES))],
        out_specs=(tok(inner), tok_t, tok(inner), tok(inner), tok(inner), tok(LANES)),
        scratch_shapes=[pltpu.VMEM((tb + HALO, inner), F32)],
        compiler_params=_params(("arbitrary", "arbitrary")),
        name="mlstm_proj",
    )(x2d, w_in[:, :inner].astype(BF16), w_in[:, inner:].astype(BF16), conv_w,
      conv_b.reshape(1, inner), wqk, wkt, wv, wg[:inner], wg[inner:2 * inner], wg[2 * inner:], bg)


def _split3(a):
    hi = a.astype(BF16)
    r1 = a - hi.astype(F32)
    mid = r1.astype(BF16)
    lo = (r1 - mid.astype(F32)).astype(BF16)
    return hi, mid, lo


CELL_CHUNK = 256


def _mlstm_cell_kernel(q_ref, kt_ref, v_ref, xc_ref, z_ref, gt_ref, x_ref,
                       skip_ref, ng_ref, wo_ref, g1_ref, b1_ref, o_ref,
                       c_ref, n_ref, m_ref, h_ref, *, tb, dh, lc):
    nh = ML_HEADS
    scale = dh ** -0.5

    @pl.when(pl.program_id(1) == 0)
    def _():
        c_ref[...] = jnp.zeros(c_ref.shape, F32)
        n_ref[...] = jnp.zeros(n_ref.shape, F32)
        m_ref[...] = jnp.zeros(m_ref.shape, F32)

    row = lax.broadcasted_iota(jnp.int32, (lc, lc), 0)
    col = lax.broadcasted_iota(jnp.int32, (lc, lc), 1)
    causal = col <= row
    tri = jnp.where(causal, 1.0, 0.0).astype(BF16)
    lane = lax.broadcasted_iota(jnp.int32, (lc, LANES), 1)

    for c in range(tb // lc):
        rows = slice(c * lc, (c + 1) * lc)
        gt = gt_ref[rows, :]
        lf = jnp.minimum(gt, 0.0) - jnp.log(1.0 + jnp.exp(-jnp.abs(gt)))
        hi, mid, lo = _split3(lf)
        bcum = _dot(tri, hi) + _dot(tri, mid) + _dot(tri, lo)
        colmat = jnp.where(lane < nh, gt, bcum)
        rowmat = colmat.T
        for h in range(nh):
            hs = slice(h * dh, (h + 1) * dh)
            i_row = rowmat[h:h + 1, :]
            b_row = rowmat[nh + h:nh + h + 1, :]
            i_col = colmat[:, h:h + 1]
            b_col = colmat[:, nh + h:nh + h + 1]
            m_prev = m_ref[h:h + 1, 0:1]
            qh = q_ref[rows, hs]
            kth = kt_ref[hs, rows]
            vh = v_ref[rows, hs]

            dmat = jnp.where(causal, b_col - b_row + i_row, -jnp.inf)
            m_inter = b_col + m_prev
            m_t = jnp.maximum(m_inter, jnp.max(dmat, axis=-1, keepdims=True))
            s = _dot(qh, kth) * scale * jnp.exp(dmat - m_t)
            si = jnp.exp(m_inter - m_t) * scale
            num = _dot(s.astype(BF16), vh) + si * _dot(qh, c_ref[h].astype(BF16))
            qn = jnp.sum(qh.astype(F32) * n_ref[h], axis=-1, keepdims=True)
            den = jnp.sum(s, axis=-1, keepdims=True) + si * qn
            h_ref[rows, hs] = num / jnp.maximum(jnp.abs(den), jnp.exp(-m_t))

            b_last = b_row[:, lc - 1:lc]
            g_row = b_last - b_row + i_row
            m_new = jnp.maximum(b_last + m_prev, jnp.max(g_row, axis=-1, keepdims=True))
            wg_row = jnp.exp(g_row - m_new)
            wg_col = jnp.exp(b_last - b_col + i_col - m_new)
            decay = jnp.exp(b_last + m_prev - m_new)
            vw = (vh.astype(F32) * wg_col).astype(BF16)
            c_ref[h] = decay * c_ref[h] + _dot(kth, vw)
            n_upd = _dot_nt(jnp.broadcast_to(wg_row, (8, lc)).astype(BF16), kth)
            n_ref[h] = decay * n_ref[h] + n_upd[0:1, :]
            m_ref[h:h + 1, :] = jnp.broadcast_to(m_new, (1, LANES))

    ng = ng_ref[...]
    for h in range(nh):
        hs = slice(h * dh, (h + 1) * dh)
        h_ref[:, hs] = _layer_norm(h_ref[:, hs], ng[:, hs])
    z = z_ref[...].astype(F32)
    hg = (h_ref[...] + skip_ref[...] * xc_ref[...].astype(F32)) * (z * _sigmoid(z))
    y = _dot(hg.astype(BF16), wo_ref[...])
    o_ref[...] = _layer_norm(ALPHA * x_ref[...] + y, g1_ref[...], b1_ref[...])


def _mlstm_cell_layer(q, kt, v, xc, z, gt, x2d, skip, norm_g, w_out, g1, b1, *, batch, tb):
    t, inner = q.shape
    d = x2d.shape[1]
    dh = inner // ML_HEADS
    seq = t // batch
    assert seq % tb == 0 and tb % CELL_CHUNK == 0
    nblk = seq // tb
    row = lambda a: a.reshape(1, -1)
    tok = lambda w: pl.BlockSpec((tb, w), lambda b, j: (b * nblk + j, 0))
    tok_t = pl.BlockSpec((inner, tb), lambda b, j: (0, b * nblk + j))
    return pl.pallas_call(
        functools.partial(_mlstm_cell_kernel, tb=tb, dh=dh, lc=CELL_CHUNK),
        out_shape=jax.ShapeDtypeStruct((t, d), F32),
        grid=(batch, nblk),
        in_specs=[tok(inner), tok_t, tok(inner), tok(inner), tok(inner), tok(LANES), tok(d),
                  _resident((1, inner)), _resident((1, inner)), _resident((inner, d)),
                  _resident((1, d)), _resident((1, d))],
        out_specs=tok(d),
        scratch_shapes=[pltpu.VMEM((ML_HEADS, dh, dh), F32),
                        pltpu.VMEM((ML_HEADS, 1, dh), F32),
                        pltpu.VMEM((8, LANES), F32),
                        pltpu.VMEM((tb, inner), F32)],
        compiler_params=_params(("arbitrary", "arbitrary")),
        name="mlstm_cell",
    )(q, kt, v, xc, z, gt, x2d, row(skip), row(norm_g), w_out.astype(BF16), row(g1), row(b1))


def kernel(x, p, gm_w_in, gm_ln_g, gm_ln_b, gm_w_s, gm_b_s, gm_w_out, ml_w_in, ml_conv_w, ml_conv_b, ml_w_q, ml_w_k, ml_w_v, ml_w_gates, ml_b_gates, ml_skip, ml_norm_g, ml_w_out, ln1_g, ln1_b, ln2_g, ln2_b, ffn_w_gate, ffn_w_up, ffn_w_down, ple_w_proj, ple_w_gate, ple_b_gate):
    batch, seq, d = x.shape
    t = batch * seq
    xs = x.reshape(t, d)
    ps = p.reshape(p.shape[0], t, p.shape[-1])

    def ffn(xs, i):
        return _ffn_layer(xs, ps[i], ffn_w_gate[i], ffn_w_up[i], ffn_w_down[i],
                          ln2_g[i], ln2_b[i], ple_w_gate[i], ple_b_gate[i], ple_w_proj[i], tb=512)

    xs = _gmlp_layer(xs, gm_w_in[0], gm_ln_g[0], gm_ln_b[0], gm_w_s[0], gm_b_s[0], gm_w_out[0],
                     ln1_g[0], ln1_b[0], tb=512)
    xs = ffn(xs, 0)
    q, kt, v, xc, z, gt = _mlstm_proj(xs, ml_w_in[0], ml_conv_w[0], ml_conv_b[0], ml_w_q[0],
                                      ml_w_k[0], ml_w_v[0], ml_w_gates[0], ml_b_gates[0],
                                      batch=batch, tb=512)
    xs = _mlstm_cell_layer(q, kt, v, xc, z, gt, xs, ml_skip[0], ml_norm_g[0], ml_w_out[0],
                           ln1_g[1], ln1_b[1], batch=batch, tb=256)
    xs = ffn(xs, 1)
    return xs.reshape(batch, seq, d)
```

```python
import functools

import jax
import jax.numpy as jnp
from jax import lax
from jax.experimental import pallas as pl
from jax.experimental.pallas import tpu as pltpu

F32 = jnp.float32
BF16 = jnp.bfloat16

CHUNK = 128
LN_EPS = 1e-5
DEPTH = 2
ALPHA = (2 * DEPTH) ** 0.25
GM_GROUPS = 8
ML_HEADS = 4
ML_CONV = 4
ML_QKV_BLOCK = 4

LANES = 128
MXU_TILE = 256
VMEM_LIMIT_BYTES = 56 * 1024 * 1024


def _dot(a, b):
    return jnp.dot(a, b, preferred_element_type=F32)


def _sigmoid(x):
    return 1.0 / (1.0 + jnp.exp(-x))


def _gelu_tanh(x):
    c = 0.7978845608028654
    return 0.5 * x * (1.0 + jnp.tanh(c * (x + 0.044715 * (x * x * x))))


def _layer_norm(x, g, b=None):
    mu = jnp.mean(x, axis=-1, keepdims=True)
    xc = x - mu
    var = jnp.mean(xc * xc, axis=-1, keepdims=True)
    y = xc * lax.rsqrt(var + LN_EPS) * g
    if b is not None:
        y = y + b
    return y


def _resident(shape, layer=None):
    nd = len(shape)
    if layer is None:
        return pl.BlockSpec(shape, lambda *_: (0,) * nd, pipeline_mode=pl.Buffered(1))
    return pl.BlockSpec((None,) + tuple(shape), lambda *_: (layer,) + (0,) * nd,
                        pipeline_mode=pl.Buffered(1))


def _params(semantics):
    return pltpu.CompilerParams(dimension_semantics=semantics,
                                vmem_limit_bytes=VMEM_LIMIT_BYTES)


def _gmlp_kernel(x_ref, wi_ref, lng_ref, lnb_ref, ws_ref, bs_ref, wo_ref,
                 g1_ref, b1_ref, o_ref, vn_ref, y_ref, *, tb, half):
    gdim = half // GM_GROUPS
    x = x_ref[...]
    xb = x.astype(BF16)
    v = _gelu_tanh(_dot(xb, wi_ref[:, half:]))
    vn_ref[...] = _layer_norm(v, lng_ref[...], lnb_ref[...]).astype(BF16)

    row = lax.broadcasted_iota(jnp.int32, (CHUNK, CHUNK), 0)
    col = lax.broadcasted_iota(jnp.int32, (CHUNK, CHUNK), 1)
    causal = col <= row
    bs = bs_ref[...]
    pair = 2 * gdim
    for j in range(GM_GROUPS // 2):
        u2 = _gelu_tanh(_dot(xb, wi_ref[:, j * pair:(j + 1) * pair]))
        for gg in range(2):
            g = 2 * j + gg
            wsg = jnp.where(causal, ws_ref[g], 0.0).astype(BF16)
            bg = bs[:, g:g + 1]
            for c in range(tb // CHUNK):
                rows = slice(c * CHUNK, (c + 1) * CHUNK)
                sv = _dot(wsg, vn_ref[rows, g * gdim:(g + 1) * gdim]) + bg
                y_ref[rows, g * gdim:(g + 1) * gdim] = (
                    u2[rows, gg * gdim:(gg + 1) * gdim] * sv).astype(BF16)
    y = _dot(y_ref[...], wo_ref[...])
    o_ref[...] = _layer_norm(ALPHA * x + y, g1_ref[...], b1_ref[...])


def _gmlp_layer(x2d, w_in, ln_g, ln_b, w_s, b_s, w_out, g1, b1, *, layer, tb):
    t, d = x2d.shape
    half = w_in.shape[1] // 2
    assert t % tb == 0 and tb % CHUNK == 0 and half % (GM_GROUPS * LANES) == 0
    row = lambda a: a.reshape(1, -1)
    tok = pl.BlockSpec((tb, d), lambda i: (i, 0))
    return pl.pallas_call(
        functools.partial(_gmlp_kernel, tb=tb, half=half),
        out_shape=jax.ShapeDtypeStruct((t, d), F32),
        grid=(t // tb,),
        in_specs=[tok, _resident((d, 2 * half)),
                  _resident((1, half)), _resident((1, half)),
                  _resident(w_s.shape), _resident((CHUNK, GM_GROUPS)),
                  _resident((half, d)), _resident((1, d), layer), _resident((1, d), layer)],
        out_specs=tok,
        scratch_shapes=[pltpu.VMEM((tb, half), BF16), pltpu.VMEM((tb, half), BF16)],
        compiler_params=_params(("arbitrary",)),
        name="gmlp_layer",
    )(x2d, w_in.astype(BF16), row(ln_g), row(ln_b), w_s, b_s.T, w_out.astype(BF16), g1, b1)


def _ffn_kernel(x_ref, p_ref, wg_ref, wu_ref, wd_ref, g2_ref, b2_ref,
                wpg_ref, bpg_ref, wpp_ref, o_ref):
    x = x_ref[...]
    xb = x.astype(BF16)
    g = _dot(xb, wg_ref[...])
    u = _dot(xb, wu_ref[...])
    h = (g * _sigmoid(g) * u).astype(BF16)
    y = _dot(h, wd_ref[...])
    x2 = _layer_norm(ALPHA * x + y, g2_ref[...], b2_ref[...])
    gate = _sigmoid(_dot(x2.astype(BF16), wpg_ref[...]) + bpg_ref[...])
    pp = _dot(p_ref[...].astype(BF16), wpp_ref[...])
    o_ref[...] = x2 + gate * pp


def _ffn_layer(x2d, p3d, w_gate, w_up, w_down, g2, b2, wpg, bpg, wpp, *, layer, tb):
    t, d = x2d.shape
    dff = w_gate.shape[-1]
    dp = p3d.shape[-1]
    assert t % tb == 0
    tok = pl.BlockSpec((tb, d), lambda i: (i, 0))
    return pl.pallas_call(
        _ffn_kernel,
        out_shape=jax.ShapeDtypeStruct((t, d), F32),
        grid=(t // tb,),
        in_specs=[tok, pl.BlockSpec((None, tb, dp), lambda i: (layer, i, 0)),
                  _resident((d, dff), layer), _resident((d, dff), layer), _resident((dff, d), layer),
                  _resident((1, d), layer), _resident((1, d), layer),
                  _resident((d, d), layer), _resident((1, d), layer), _resident((dp, d), layer)],
        out_specs=tok,
        compiler_params=_params(("arbitrary",)),
        name="ffn_layer",
    )(x2d, p3d, w_gate, w_up, w_down, g2, b2, wpg, bpg, wpp)


HALO = 8


def _dot_nt(a, b):
    return lax.dot_general(a, b, (((1,), (1,)), ((), ())), preferred_element_type=F32)


def _mlstm_proj_kernel(x_ref, wi_ref, cw_ref, cb_ref, wqk_ref, wkt_ref, wv_ref, wg_ref, bg_ref,
                       q_ref, kt_ref, v_ref, xc_ref, z_ref, gt_ref, ext_ref, *, tb, inner):
    xb = x_ref[...].astype(BF16)
    z_ref[...] = _dot(xb, wi_ref[:, inner:]).astype(BF16)
    xm = _dot(xb, wi_ref[:, :inner])

    @pl.when(pl.program_id(1) == 0)
    def _():
        ext_ref[0:HALO, :] = jnp.zeros((HALO, inner), F32)

    ext_ref[HALO:HALO + tb, :] = xm
    cw = cw_ref[...]
    conv = cw[ML_CONV - 1:ML_CONV, :] * xm + cb_ref[...]
    for j in range(1, ML_CONV):
        conv = conv + cw[ML_CONV - 1 - j:ML_CONV - j, :] * ext_ref[HALO - j:HALO - j + tb, :]
    ext_ref[0:HALO, :] = ext_ref[tb:tb + HALO, :]
    xc = conv * _sigmoid(conv)
    xcb = xc.astype(BF16)
    xmb = xm.astype(BF16)
    xc_ref[...] = xcb

    gates = jnp.broadcast_to(bg_ref[...], (tb, LANES))
    for g in range(inner // MXU_TILE):
        sl = slice(g * MXU_TILE, (g + 1) * MXU_TILE)
        qk = _dot(xcb[:, sl], wqk_ref[g])
        qg = qk[:, :MXU_TILE].astype(BF16)
        kg = qk[:, MXU_TILE:].astype(BF16)
        vg = _dot(xmb[:, sl], wv_ref[g]).astype(BF16)
        q_ref[:, sl] = qg
        kt_ref[sl, :] = _dot_nt(wkt_ref[g], xcb[:, sl]).astype(BF16)
        v_ref[:, sl] = vg
        gates = (gates + _dot(qg, wg_ref[sl, :])
                 + _dot(kg, wg_ref[inner + g * MXU_TILE:inner + (g + 1) * MXU_TILE, :])
                 + _dot(vg, wg_ref[2 * inner + g * MXU_TILE:2 * inner + (g + 1) * MXU_TILE, :]))
    gt_ref[...] = gates


def _block_diag_tiles(w, tile):
    nb, c, d = w.shape
    assert c == d and tile % c == 0 and nb % (tile // c) == 0
    cols = lax.broadcasted_iota(jnp.int32, (d, tile), 1)
    expand = (cols % d == lax.broadcasted_iota(jnp.int32, (d, tile), 0)).astype(F32)
    tiled = jnp.dot(w.reshape(nb * c, d), expand, precision=lax.Precision.HIGHEST)
    r = lax.broadcasted_iota(jnp.int32, (tile, tile), 0) // c
    s = lax.broadcasted_iota(jnp.int32, (tile, tile), 1) // d
    return jnp.where(r == s, tiled.reshape(nb * c // tile, tile, tile), 0.0).astype(BF16)


def _mlstm_proj(x2d, w_in, conv_w, conv_b, w_q, w_k, w_v, w_gates, b_gates, *, batch, tb):
    t, d = x2d.shape
    inner = w_in.shape[1] // 2
    seq = t // batch
    assert seq % tb == 0 and inner % MXU_TILE == 0 and tb >= HALO
    nblk = seq // tb
    ntile = inner // MXU_TILE
    wqk = jnp.concatenate([_block_diag_tiles(w_q, MXU_TILE), _block_diag_tiles(w_k, MXU_TILE)],
                          axis=-1)
    wkt = _block_diag_tiles(jnp.swapaxes(w_k, 1, 2), MXU_TILE)
    wv = _block_diag_tiles(w_v, MXU_TILE)
    ng = w_gates.shape[1]
    wg = jnp.pad(w_gates, ((0, 0), (0, LANES - ng))).astype(BF16)
    bg = jnp.pad(b_gates, (0, LANES - ng)).reshape(1, LANES)
    tok = lambda w: pl.BlockSpec((tb, w), lambda b, j: (b * nblk + j, 0))
    tok_t = pl.BlockSpec((None, inner, tb), lambda b, j: (b, 0, j))
    act = jax.ShapeDtypeStruct((t, inner), BF16)
    act_t = jax.ShapeDtypeStruct((batch, inner, seq), BF16)
    return pl.pallas_call(
        functools.partial(_mlstm_proj_kernel, tb=tb, inner=inner),
        out_shape=(act, act_t, act, act, act, jax.ShapeDtypeStruct((t, LANES), F32)),
        grid=(batch, nblk),
        in_specs=[tok(d), _resident((d, 2 * inner)),
                  _resident((ML_CONV, inner)), _resident((1, inner)),
                  _resident((ntile, MXU_TILE, 2 * MXU_TILE)), _resident((ntile, MXU_TILE, MXU_TILE)),
                  _resident((ntile, MXU_TILE, MXU_TILE)),
                  _resident((3 * inner, LANES)), _resident((1, LANES))],
        out_specs=(tok(inner), tok_t, tok(inner), tok(inner), tok(inner), tok(LANES)),
        scratch_shapes=[pltpu.VMEM((tb + HALO, inner), F32)],
        compiler_params=_params(("arbitrary", "arbitrary")),
        name="mlstm_proj",
    )(x2d, w_in.astype(BF16), conv_w, conv_b.reshape(1, inner), wqk, wkt, wv, wg, bg)


def _split3(a):
    hi = a.astype(BF16)
    r1 = a - hi.astype(F32)
    mid = r1.astype(BF16)
    lo = (r1 - mid.astype(F32)).astype(BF16)
    return hi, mid, lo


CELL_CHUNK = 256


def _mlstm_cell_kernel(q_ref, kt_ref, v_ref, xc_ref, z_ref, gt_ref, x_ref,
                       skip_ref, ng_ref, wo_ref, g1_ref, b1_ref, o_ref,
                       c_ref, n_ref, m_ref, hg_ref, *, nb, tb, dh, lc):
    nh = ML_HEADS
    scale = dh ** -0.5

    @pl.when(pl.program_id(1) == 0)
    def _():
        c_ref[...] = jnp.zeros(c_ref.shape, F32)
        n_ref[...] = jnp.zeros(n_ref.shape, F32)
        m_ref[...] = jnp.zeros(m_ref.shape, F32)

    row = lax.broadcasted_iota(jnp.int32, (lc, lc), 0)
    col = lax.broadcasted_iota(jnp.int32, (lc, lc), 1)
    causal = col <= row
    tri = jnp.where(causal, 1.0, 0.0).astype(BF16)
    lane = lax.broadcasted_iota(jnp.int32, (lc, LANES), 1)

    for c in range(tb // lc):
        rows = slice(c * lc, (c + 1) * lc)
        for b in range(nb):
            gt = gt_ref[b, rows, :]
            lf = jnp.minimum(gt, 0.0) - jnp.log(1.0 + jnp.exp(-jnp.abs(gt)))
            hi, mid, lo = _split3(lf)
            bcum = _dot(tri, hi) + _dot(tri, mid) + _dot(tri, lo)
            colmat = jnp.where(lane < nh, gt, bcum)
            rowmat = colmat.T
            for h in range(nh):
                st = b * nh + h
                hs = slice(h * dh, (h + 1) * dh)
                i_row = rowmat[h:h + 1, :]
                b_row = rowmat[nh + h:nh + h + 1, :]
                i_col = colmat[:, h:h + 1]
                b_col = colmat[:, nh + h:nh + h + 1]
                m_prev = m_ref[st:st + 1, 0:1]
                qh = q_ref[b, rows, hs]
                kth = kt_ref[b, hs, rows]
                vh = v_ref[b, rows, hs]

                dmat = jnp.where(causal, b_col - b_row + i_row, -jnp.inf)
                m_inter = b_col + m_prev
                m_t = jnp.maximum(m_inter, jnp.max(dmat, axis=-1, keepdims=True))
                s = _dot(qh, kth) * scale * jnp.exp(dmat - m_t)
                si = jnp.exp(m_inter - m_t) * scale
                qs = qh.astype(F32) * si
                num = _dot(s.astype(BF16), vh) + _dot(qs.astype(BF16), c_ref[st].astype(BF16))
                den = (jnp.sum(s, axis=-1, keepdims=True)
                       + jnp.sum(qs * n_ref[st], axis=-1, keepdims=True))
                hval = num / jnp.maximum(jnp.abs(den), jnp.exp(-m_t))
                z = z_ref[b, rows, hs].astype(F32)
                hg = ((_layer_norm(hval, ng_ref[:, hs])
                       + skip_ref[:, hs] * xc_ref[b, rows, hs].astype(F32)) * (z * _sigmoid(z)))
                hg_ref[b, rows, hs] = hg.astype(BF16)

                b_last = b_row[:, lc - 1:lc]
                g_row = b_last - b_row + i_row
                m_new = jnp.maximum(b_last + m_prev, jnp.max(g_row, axis=-1, keepdims=True))
                wg_row = jnp.exp(g_row - m_new)
                wg_col = jnp.exp(b_last - b_col + i_col - m_new)
                decay = jnp.exp(b_last + m_prev - m_new)
                vw = (vh.astype(F32) * wg_col).astype(BF16)
                c_ref[st] = decay * c_ref[st] + _dot(kth, vw)
                n_upd = _dot_nt(jnp.broadcast_to(wg_row, (8, lc)).astype(BF16), kth)
                n_ref[st] = decay * n_ref[st] + n_upd[0:1, :]
                m_ref[st:st + 1, :] = jnp.broadcast_to(m_new, (1, LANES))

            y = _dot(hg_ref[b, rows, :], wo_ref[...])
            o_ref[b, rows, :] = _layer_norm(ALPHA * x_ref[b, rows, :] + y, g1_ref[...], b1_ref[...])


def _mlstm_cell_layer(q, kt, v, xc, z, gt, x3d, skip, norm_g, w_out, g1, b1, *, layer, nb, tb):
    batch, seq, inner = q.shape
    d = x3d.shape[-1]
    dh = inner // ML_HEADS
    assert seq % tb == 0 and tb % CELL_CHUNK == 0 and batch % nb == 0
    row = lambda a: a.reshape(1, -1)
    tok = lambda w: pl.BlockSpec((nb, tb, w), lambda bi, j: (bi, j, 0))
    tok_t = pl.BlockSpec((nb, inner, tb), lambda bi, j: (bi, 0, j))
    nstate = nb * ML_HEADS
    return pl.pallas_call(
        functools.partial(_mlstm_cell_kernel, nb=nb, tb=tb, dh=dh, lc=CELL_CHUNK),
        out_shape=jax.ShapeDtypeStruct((batch, seq, d), F32),
        grid=(batch // nb, seq // tb),
        in_specs=[tok(inner), tok_t, tok(inner), tok(inner), tok(inner), tok(LANES), tok(d),
                  _resident((1, inner)), _resident((1, inner)), _resident((inner, d)),
                  _resident((1, d), layer), _resident((1, d), layer)],
        out_specs=tok(d),
        scratch_shapes=[pltpu.VMEM((nstate, dh, dh), F32),
                        pltpu.VMEM((nstate, 1, dh), F32),
                        pltpu.VMEM((-(-nstate // 8) * 8, LANES), F32),
                        pltpu.VMEM((nb, tb, inner), BF16)],
        compiler_params=_params(("arbitrary", "arbitrary")),
        name="mlstm_cell",
    )(q, kt, v, xc, z, gt, x3d, row(skip), row(norm_g), w_out.astype(BF16), g1, b1)


def kernel(x, p, gm_w_in, gm_ln_g, gm_ln_b, gm_w_s, gm_b_s, gm_w_out, ml_w_in, ml_conv_w, ml_conv_b, ml_w_q, ml_w_k, ml_w_v, ml_w_gates, ml_b_gates, ml_skip, ml_norm_g, ml_w_out, ln1_g, ln1_b, ln2_g, ln2_b, ffn_w_gate, ffn_w_up, ffn_w_down, ple_w_proj, ple_w_gate, ple_b_gate):
    batch, seq, d = x.shape
    t = batch * seq
    xs = x.reshape(t, d)
    ps = p.reshape(p.shape[0], t, p.shape[-1])
    vec = lambda a: a.reshape(a.shape[0], 1, a.shape[-1])
    ln1g, ln1b, ln2g, ln2b, pbg = vec(ln1_g), vec(ln1_b), vec(ln2_g), vec(ln2_b), vec(ple_b_gate)
    ffn_w = [w.astype(BF16) for w in (ffn_w_gate, ffn_w_up, ffn_w_down)]
    wpg, wpp = ple_w_gate.astype(BF16), ple_w_proj.astype(BF16)

    def ffn(xs, i):
        return _ffn_layer(xs, ps, *ffn_w, ln2g, ln2b, wpg, pbg, wpp, layer=i, tb=512)

    xs = _gmlp_layer(xs, gm_w_in[0], gm_ln_g[0], gm_ln_b[0], gm_w_s[0], gm_b_s[0], gm_w_out[0],
                     ln1g, ln1b, layer=0, tb=512)
    xs = ffn(xs, 0)
    q, kt, v, xc, z, gt = _mlstm_proj(xs, ml_w_in[0], ml_conv_w[0], ml_conv_b[0], ml_w_q[0],
                                      ml_w_k[0], ml_w_v[0], ml_w_gates[0], ml_b_gates[0],
                                      batch=batch, tb=512)
    seq3 = lambda a: a.reshape(batch, seq, a.shape[-1])
    xs = _mlstm_cell_layer(seq3(q), kt, seq3(v), seq3(xc), seq3(z), seq3(gt), seq3(xs),
                           ml_skip[0], ml_norm_g[0], ml_w_out[0], ln1g, ln1b, layer=1, nb=1, tb=256)
    xs = ffn(xs.reshape(t, d), 1)
    return xs.reshape(batch, seq, d)
```

```python
import functools

import jax
import jax.numpy as jnp
from jax import lax
from jax.experimental import pallas as pl
from jax.experimental.pallas import tpu as pltpu

F32 = jnp.float32
BF16 = jnp.bfloat16

CHUNK = 128
LN_EPS = 1e-5
DEPTH = 2
ALPHA = (2 * DEPTH) ** 0.25
GM_GROUPS = 8
ML_HEADS = 4
ML_CONV = 4
ML_QKV_BLOCK = 4

LANES = 128
MXU_TILE = 256
VMEM_LIMIT_BYTES = 56 * 1024 * 1024


def _dot(a, b):
    return jnp.dot(a, b, preferred_element_type=F32)


def _sigmoid(x):
    return 1.0 / (1.0 + jnp.exp(-x))


def _gelu_tanh(x):
    c = 0.7978845608028654
    hx = 0.5 * x
    return hx + hx * jnp.tanh(x * (c + (c * 0.044715) * (x * x)))


def _layer_norm(x, g, b=None):
    mu = jnp.mean(x, axis=-1, keepdims=True)
    xc = x - mu
    var = jnp.mean(xc * xc, axis=-1, keepdims=True)
    y = xc * lax.rsqrt(var + LN_EPS) * g
    if b is not None:
        y = y + b
    return y


def _resident(shape, layer=None):
    nd = len(shape)
    if layer is None:
        return pl.BlockSpec(shape, lambda *_: (0,) * nd, pipeline_mode=pl.Buffered(1))
    return pl.BlockSpec((None,) + tuple(shape), lambda *_: (layer,) + (0,) * nd,
                        pipeline_mode=pl.Buffered(1))


def _params(semantics):
    return pltpu.CompilerParams(dimension_semantics=semantics,
                                vmem_limit_bytes=VMEM_LIMIT_BYTES)


def _gmlp_kernel(x_ref, wi_ref, lng_ref, lnb_ref, ws_ref, bs_ref, wo_ref,
                 g1_ref, b1_ref, o_ref, vn_ref, y_ref, *, tb, half):
    gdim = half // GM_GROUPS
    x = x_ref[...]
    xb = x.astype(BF16)
    v = _gelu_tanh(_dot(xb, wi_ref[:, half:]))
    vn_ref[...] = _layer_norm(v, lng_ref[...], lnb_ref[...]).astype(BF16)

    row = lax.broadcasted_iota(jnp.int32, (CHUNK, CHUNK), 0)
    col = lax.broadcasted_iota(jnp.int32, (CHUNK, CHUNK), 1)
    causal = col <= row
    bs = bs_ref[...]
    pair = 2 * gdim
    for j in range(GM_GROUPS // 2):
        u2 = _gelu_tanh(_dot(xb, wi_ref[:, j * pair:(j + 1) * pair]))
        for gg in range(2):
            g = 2 * j + gg
            wsg = jnp.where(causal, ws_ref[g], 0.0).astype(BF16)
            bg = bs[:, g:g + 1]
            for c in range(tb // CHUNK):
                rows = slice(c * CHUNK, (c + 1) * CHUNK)
                sv = _dot(wsg, vn_ref[rows, g * gdim:(g + 1) * gdim]) + bg
                y_ref[rows, g * gdim:(g + 1) * gdim] = (
                    u2[rows, gg * gdim:(gg + 1) * gdim] * sv).astype(BF16)
    for r in range(2):
        rows = slice(r * (tb // 2), (r + 1) * (tb // 2))
        y = _dot(y_ref[rows, :], wo_ref[...])
        o_ref[rows, :] = _layer_norm(ALPHA * x_ref[rows, :] + y, g1_ref[...], b1_ref[...])


def _gmlp_layer(x2d, w_in, ln_g, ln_b, w_s, b_s, w_out, g1, b1, *, layer, tb):
    t, d = x2d.shape
    half = w_in.shape[1] // 2
    assert t % tb == 0 and tb % CHUNK == 0 and half % (GM_GROUPS * LANES) == 0
    row = lambda a: a.reshape(1, -1)
    tok = pl.BlockSpec((tb, d), lambda i: (i, 0))
    return pl.pallas_call(
        functools.partial(_gmlp_kernel, tb=tb, half=half),
        out_shape=jax.ShapeDtypeStruct((t, d), F32),
        grid=(t // tb,),
        in_specs=[tok, _resident((d, 2 * half)),
                  _resident((1, half)), _resident((1, half)),
                  _resident(w_s.shape), _resident((CHUNK, GM_GROUPS)),
                  _resident((half, d)), _resident((1, d), layer), _resident((1, d), layer)],
        out_specs=tok,
        scratch_shapes=[pltpu.VMEM((tb, half), BF16), pltpu.VMEM((tb, half), BF16)],
        compiler_params=_params(("arbitrary",)),
        name="gmlp_layer",
    )(x2d, w_in.astype(BF16), row(ln_g), row(ln_b), w_s, b_s.T, w_out.astype(BF16), g1, b1)


def _ffn_kernel(x_ref, p_ref, wg_ref, wu_ref, wd_ref, g2_ref, b2_ref,
                wpg_ref, bpg_ref, wpp_ref, o_ref):
    x = x_ref[...]
    xb = x.astype(BF16)
    g = _dot(xb, wg_ref[...])
    u = _dot(xb, wu_ref[...])
    h = (g * _sigmoid(g) * u).astype(BF16)
    y = _dot(h, wd_ref[...])
    x2 = _layer_norm(ALPHA * x + y, g2_ref[...], b2_ref[...])
    gate = _sigmoid(_dot(x2.astype(BF16), wpg_ref[...]) + bpg_ref[...])
    pp = _dot(p_ref[...].astype(BF16), wpp_ref[...])
    o_ref[...] = x2 + gate * pp


def _ffn_layer(x2d, p3d, w_gate, w_up, w_down, g2, b2, wpg, bpg, wpp, *, layer, tb):
    t, d = x2d.shape
    dff = w_gate.shape[-1]
    dp = p3d.shape[-1]
    assert t % tb == 0
    tok = pl.BlockSpec((tb, d), lambda i: (i, 0))
    return pl.pallas_call(
        _ffn_kernel,
        out_shape=jax.ShapeDtypeStruct((t, d), F32),
        grid=(t // tb,),
        in_specs=[tok, pl.BlockSpec((None, tb, dp), lambda i: (layer, i, 0)),
                  _resident((d, dff), layer), _resident((d, dff), layer), _resident((dff, d), layer),
                  _resident((1, d), layer), _resident((1, d), layer),
                  _resident((d, d), layer), _resident((1, d), layer), _resident((dp, d), layer)],
        out_specs=tok,
        compiler_params=_params(("arbitrary",)),
        name="ffn_layer",
    )(x2d, p3d, w_gate, w_up, w_down, g2, b2, wpg, bpg, wpp)


HALO = 8


def _dot_nt(a, b):
    return lax.dot_general(a, b, (((1,), (1,)), ((), ())), preferred_element_type=F32)


def _mlstm_proj_kernel(x_ref, wi_ref, cw_ref, cb_ref, wqk_ref, wkt_ref, wv_ref, wg_ref, bg_ref,
                       q_ref, kt_ref, v_ref, xc_ref, z_ref, gt_ref, ext_ref, *, tb, inner):
    xb = x_ref[...].astype(BF16)

    @pl.when(pl.program_id(1) == 0)
    def _():
        ext_ref[0:HALO, :] = jnp.zeros((HALO, inner), F32)

    xm = _dot(xb, wi_ref[:, :inner])
    ext_ref[HALO:HALO + tb, :] = xm
    z_ref[...] = _dot(xb, wi_ref[:, inner:]).astype(BF16)
    cw = cw_ref[...]
    conv = cw[ML_CONV - 1:ML_CONV, :] * xm + cb_ref[...]
    for j in range(1, ML_CONV):
        conv = conv + cw[ML_CONV - 1 - j:ML_CONV - j, :] * ext_ref[HALO - j:HALO - j + tb, :]
    ext_ref[0:HALO, :] = ext_ref[tb:tb + HALO, :]
    xc = conv * _sigmoid(conv)
    xcb = xc.astype(BF16)
    xmb = xm.astype(BF16)
    xc_ref[...] = xcb

    gates = jnp.broadcast_to(bg_ref[...], (tb, LANES))
    for g in range(inner // MXU_TILE):
        sl = slice(g * MXU_TILE, (g + 1) * MXU_TILE)
        qk = _dot(xcb[:, sl], wqk_ref[g])
        qg = qk[:, :MXU_TILE].astype(BF16)
        kg = qk[:, MXU_TILE:].astype(BF16)
        vg = _dot(xmb[:, sl], wv_ref[g]).astype(BF16)
        q_ref[:, sl] = qg
        kt_ref[sl, :] = _dot_nt(wkt_ref[g], xcb[:, sl]).astype(BF16)
        v_ref[:, sl] = vg
        gates = (gates + _dot(qg, wg_ref[sl, :])
                 + _dot(kg, wg_ref[inner + g * MXU_TILE:inner + (g + 1) * MXU_TILE, :])
                 + _dot(vg, wg_ref[2 * inner + g * MXU_TILE:2 * inner + (g + 1) * MXU_TILE, :]))
    gt_ref[...] = gates


def _block_diag_tiles(w, tile):
    nb, c, d = w.shape
    assert c == d and tile % c == 0 and nb % (tile // c) == 0
    cols = lax.broadcasted_iota(jnp.int32, (d, tile), 1)
    expand = (cols % d == lax.broadcasted_iota(jnp.int32, (d, tile), 0)).astype(F32)
    tiled = jnp.dot(w.reshape(nb * c, d), expand, precision=lax.Precision.HIGHEST)
    r = lax.broadcasted_iota(jnp.int32, (tile, tile), 0) // c
    s = lax.broadcasted_iota(jnp.int32, (tile, tile), 1) // d
    return jnp.where(r == s, tiled.reshape(nb * c // tile, tile, tile), 0.0).astype(BF16)


def _mlstm_proj(x2d, w_in, conv_w, conv_b, w_q, w_k, w_v, w_gates, b_gates, *, batch, tb):
    t, d = x2d.shape
    inner = w_in.shape[1] // 2
    seq = t // batch
    assert seq % tb == 0 and inner % MXU_TILE == 0 and tb >= HALO
    nblk = seq // tb
    ntile = inner // MXU_TILE
    wqk = jnp.concatenate([_block_diag_tiles(w_q, MXU_TILE), _block_diag_tiles(w_k, MXU_TILE)],
                          axis=-1)
    wkt = _block_diag_tiles(jnp.swapaxes(w_k, 1, 2), MXU_TILE)
    wv = _block_diag_tiles(w_v, MXU_TILE)
    ng = w_gates.shape[1]
    wg = jnp.pad(w_gates, ((0, 0), (0, LANES - ng))).astype(BF16)
    bg = jnp.pad(b_gates, (0, LANES - ng)).reshape(1, LANES)
    tok = lambda w: pl.BlockSpec((tb, w), lambda b, j: (b * nblk + j, 0))
    tok_t = pl.BlockSpec((None, inner, tb), lambda b, j: (b, 0, j))
    act = jax.ShapeDtypeStruct((t, inner), BF16)
    act_t = jax.ShapeDtypeStruct((batch, inner, seq), BF16)
    return pl.pallas_call(
        functools.partial(_mlstm_proj_kernel, tb=tb, inner=inner),
        out_shape=(act, act_t, act, act, act, jax.ShapeDtypeStruct((t, LANES), F32)),
        grid=(batch, nblk),
        in_specs=[tok(d), _resident((d, 2 * inner)),
                  _resident((ML_CONV, inner)), _resident((1, inner)),
                  _resident((ntile, MXU_TILE, 2 * MXU_TILE)), _resident((ntile, MXU_TILE, MXU_TILE)),
                  _resident((ntile, MXU_TILE, MXU_TILE)),
                  _resident((3 * inner, LANES)), _resident((1, LANES))],
        out_specs=(tok(inner), tok_t, tok(inner), tok(inner), tok(inner), tok(LANES)),
        scratch_shapes=[pltpu.VMEM((tb + HALO, inner), F32)],
        compiler_params=_params(("arbitrary", "arbitrary")),
        name="mlstm_proj",
    )(x2d, w_in.astype(BF16), conv_w, conv_b.reshape(1, inner), wqk, wkt, wv, wg, bg)


def _split3(a):
    hi = a.astype(BF16)
    r1 = a - hi.astype(F32)
    mid = r1.astype(BF16)
    lo = (r1 - mid.astype(F32)).astype(BF16)
    return hi, mid, lo


CELL_CHUNK = 256


def _mlstm_cell_kernel(q_ref, kt_ref, v_ref, xc_ref, z_ref, gt_ref, x_ref,
                       skip_ref, ng_ref, wo_ref, g1_ref, b1_ref, o_ref,
                       c_ref, n_ref, m_ref, hg_ref, *, nb, tb, dh, lc):
    nh = ML_HEADS
    scale = dh ** -0.5

    @pl.when(pl.program_id(1) == 0)
    def _():
        c_ref[...] = jnp.zeros(c_ref.shape, F32)
        n_ref[...] = jnp.zeros(n_ref.shape, F32)
        m_ref[...] = jnp.zeros(m_ref.shape, F32)

    row = lax.broadcasted_iota(jnp.int32, (lc, lc), 0)
    col = lax.broadcasted_iota(jnp.int32, (lc, lc), 1)
    causal = col <= row
    tri = jnp.where(causal, 1.0, 0.0).astype(BF16)
    lane = lax.broadcasted_iota(jnp.int32, (lc, LANES), 1)

    for c in range(tb // lc):
        rows = slice(c * lc, (c + 1) * lc)
        for b in range(nb):
            gt = gt_ref[b, rows, :]
            lf = jnp.minimum(gt, 0.0) - jnp.log(1.0 + jnp.exp(-jnp.abs(gt)))
            hi, mid, lo = _split3(lf)
            bcum = _dot(tri, hi) + _dot(tri, mid) + _dot(tri, lo)
            colmat = jnp.where(lane < nh, gt, bcum)
            rowmat = colmat.T
            for h in range(nh):
                st = b * nh + h
                hs = slice(h * dh, (h + 1) * dh)
                i_row = rowmat[h:h + 1, :]
                b_row = rowmat[nh + h:nh + h + 1, :]
                i_col = colmat[:, h:h + 1]
                b_col = colmat[:, nh + h:nh + h + 1]
                m_prev = m_ref[st:st + 1, 0:1]
                qh = q_ref[b, rows, hs]
                kth = kt_ref[b, hs, rows]
                vh = v_ref[b, rows, hs]

                dmat = jnp.where(causal, b_col - b_row + i_row, -jnp.inf)
                m_inter = b_col + m_prev
                m_t = jnp.maximum(m_inter, jnp.max(dmat, axis=-1, keepdims=True))
                s = _dot(qh, kth) * scale * jnp.exp(dmat - m_t)
                si = jnp.exp(m_inter - m_t) * scale
                qs = qh.astype(F32) * si
                num = _dot(s.astype(BF16), vh) + _dot(qs.astype(BF16), c_ref[st].astype(BF16))
                den = (jnp.sum(s, axis=-1, keepdims=True)
                       + jnp.sum(qs * n_ref[st], axis=-1, keepdims=True))
                hval = num / jnp.maximum(jnp.abs(den), jnp.exp(-m_t))
                z = z_ref[b, rows, hs].astype(F32)
                hg = ((_layer_norm(hval, ng_ref[:, hs])
                       + skip_ref[:, hs] * xc_ref[b, rows, hs].astype(F32)) * (z * _sigmoid(z)))
                hg_ref[b, rows, hs] = hg.astype(BF16)

                b_last = b_row[:, lc - 1:lc]
                g_row = b_last - b_row + i_row
                m_new = jnp.maximum(b_last + m_prev, jnp.max(g_row, axis=-1, keepdims=True))
                wg_row = jnp.exp(g_row - m_new)
                wg_col = jnp.exp(b_last - b_col + i_col - m_new)
                decay = jnp.exp(b_last + m_prev - m_new)
                vw = (vh.astype(F32) * wg_col).astype(BF16)
                c_ref[st] = decay * c_ref[st] + _dot(kth, vw)
                n_upd = _dot_nt(jnp.broadcast_to(wg_row, (8, lc)).astype(BF16), kth)
                n_ref[st] = decay * n_ref[st] + n_upd[0:1, :]
                m_ref[st:st + 1, :] = jnp.broadcast_to(m_new, (1, LANES))

            y = _dot(hg_ref[b, rows, :], wo_ref[...])
            o_ref[b, rows, :] = _layer_norm(ALPHA * x_ref[b, rows, :] + y, g1_ref[...], b1_ref[...])


def _mlstm_cell_layer(q, kt, v, xc, z, gt, x3d, skip, norm_g, w_out, g1, b1, *, layer, nb, tb):
    batch, seq, inner = q.shape
    d = x3d.shape[-1]
    dh = inner // ML_HEADS
    assert seq % tb == 0 and tb % CELL_CHUNK == 0 and batch % nb == 0
    row = lambda a: a.reshape(1, -1)
    tok = lambda w: pl.BlockSpec((nb, tb, w), lambda bi, j: (bi, j, 0))
    tok_t = pl.BlockSpec((nb, inner, tb), lambda bi, j: (bi, 0, j))
    nstate = nb * ML_HEADS
    return pl.pallas_call(
        functools.partial(_mlstm_cell_kernel, nb=nb, tb=tb, dh=dh, lc=CELL_CHUNK),
        out_shape=jax.ShapeDtypeStruct((batch, seq, d), F32),
        grid=(batch // nb, seq // tb),
        in_specs=[tok(inner), tok_t, tok(inner), tok(inner), tok(inner), tok(LANES), tok(d),
                  _resident((1, inner)), _resident((1, inner)), _resident((inner, d)),
                  _resident((1, d), layer), _resident((1, d), layer)],
        out_specs=tok(d),
        scratch_shapes=[pltpu.VMEM((nstate, dh, dh), F32),
                        pltpu.VMEM((nstate, 1, dh), F32),
                        pltpu.VMEM((-(-nstate // 8) * 8, LANES), F32),
                        pltpu.VMEM((nb, tb, inner), BF16)],
        compiler_params=_params(("arbitrary", "arbitrary")),
        name="mlstm_cell",
    )(q, kt, v, xc, z, gt, x3d, row(skip), row(norm_g), w_out.astype(BF16), g1, b1)


def kernel(x, p, gm_w_in, gm_ln_g, gm_ln_b, gm_w_s, gm_b_s, gm_w_out, ml_w_in, ml_conv_w, ml_conv_b, ml_w_q, ml_w_k, ml_w_v, ml_w_gates, ml_b_gates, ml_skip, ml_norm_g, ml_w_out, ln1_g, ln1_b, ln2_g, ln2_b, ffn_w_gate, ffn_w_up, ffn_w_down, ple_w_proj, ple_w_gate, ple_b_gate):
    batch, seq, d = x.shape
    t = batch * seq
    xs = x.reshape(t, d)
    ps = p.reshape(p.shape[0], t, p.shape[-1])
    vec = lambda a: a.reshape(a.shape[0], 1, a.shape[-1])
    ln1g, ln1b, ln2g, ln2b, pbg = vec(ln1_g), vec(ln1_b), vec(ln2_g), vec(ln2_b), vec(ple_b_gate)
    ffn_w = [w.astype(BF16) for w in (ffn_w_gate, ffn_w_up, ffn_w_down)]
    wpg, wpp = ple_w_gate.astype(BF16), ple_w_proj.astype(BF16)

    def ffn(xs, i):
        return _ffn_layer(xs, ps, *ffn_w, ln2g, ln2b, wpg, pbg, wpp, layer=i, tb=512)

    xs = _gmlp_layer(xs, gm_w_in[0], gm_ln_g[0], gm_ln_b[0], gm_w_s[0], gm_b_s[0], gm_w_out[0],
                     ln1g, ln1b, layer=0, tb=512)
    xs = ffn(xs, 0)
    q, kt, v, xc, z, gt = _mlstm_proj(xs, ml_w_in[0], ml_conv_w[0], ml_conv_b[0], ml_w_q[0],
                                      ml_w_k[0], ml_w_v[0], ml_w_gates[0], ml_b_gates[0],
                                      batch=batch, tb=512)
    seq3 = lambda a: a.reshape(batch, seq, a.shape[-1])
    xs = _mlstm_cell_layer(seq3(q), kt, seq3(v), seq3(xc), seq3(z), seq3(gt), seq3(xs),
                           ml_skip[0], ml_norm_g[0], ml_w_out[0], ln1g, ln1b, layer=1, nb=1, tb=256)
    xs = ffn(xs.reshape(t, d), 1)
    return xs.reshape(batch, seq, d)
```

```python
import functools

import jax
import jax.numpy as jnp
from jax import lax
from jax.experimental import pallas as pl
from jax.experimental.pallas import tpu as pltpu

F32 = jnp.float32
BF16 = jnp.bfloat16

CHUNK = 128
LN_EPS = 1e-5
DEPTH = 2
ALPHA = (2 * DEPTH) ** 0.25
GM_GROUPS = 8
ML_HEADS = 4
ML_CONV = 4
ML_QKV_BLOCK = 4

LANES = 128
MXU_TILE = 256
VMEM_LIMIT_BYTES = 56 * 1024 * 1024


def _dot(a, b):
    return jnp.dot(a, b, preferred_element_type=F32)


def _sigmoid(x):
    return 1.0 / (1.0 + jnp.exp(-x))


def _gelu_tanh(x):
    c = 0.7978845608028654
    hx = 0.5 * x
    return hx + hx * jnp.tanh(x * (c + (c * 0.044715) * (x * x)))


def _layer_norm(x, g, b=None):
    mu = jnp.mean(x, axis=-1, keepdims=True)
    xc = x - mu
    var = jnp.mean(xc * xc, axis=-1, keepdims=True)
    y = xc * lax.rsqrt(var + LN_EPS) * g
    if b is not None:
        y = y + b
    return y


def _resident(shape, layer=None):
    nd = len(shape)
    if layer is None:
        return pl.BlockSpec(shape, lambda *_: (0,) * nd, pipeline_mode=pl.Buffered(1))
    return pl.BlockSpec((None,) + tuple(shape), lambda *_: (layer,) + (0,) * nd,
                        pipeline_mode=pl.Buffered(1))


def _params(semantics):
    return pltpu.CompilerParams(dimension_semantics=semantics,
                                vmem_limit_bytes=VMEM_LIMIT_BYTES)


def _gmlp_kernel(x_ref, wi_ref, lng_ref, lnb_ref, ws_ref, bs_ref, wo_ref,
                 g1_ref, b1_ref, o_ref, vn_ref, y_ref, *, tb, half):
    gdim = half // GM_GROUPS
    x = x_ref[...]
    xb = x.astype(BF16)
    pair = 2 * gdim
    u_pair = lambda j: _gelu_tanh(_dot(xb, wi_ref[:, j * pair:(j + 1) * pair]))
    v = _gelu_tanh(_dot(xb, wi_ref[:, half:]))
    u_next = u_pair(0)
    vn_ref[...] = _layer_norm(v, lng_ref[...], lnb_ref[...]).astype(BF16)

    row = lax.broadcasted_iota(jnp.int32, (CHUNK, CHUNK), 0)
    col = lax.broadcasted_iota(jnp.int32, (CHUNK, CHUNK), 1)
    causal = col <= row
    bs = bs_ref[...]
    for j in range(GM_GROUPS // 2):
        u2 = u_next if j == 0 else u_pair(j)
        for gg in range(2):
            g = 2 * j + gg
            wsg = jnp.where(causal, ws_ref[g], 0.0).astype(BF16)
            bg = bs[:, g:g + 1]
            for c in range(tb // CHUNK):
                rows = slice(c * CHUNK, (c + 1) * CHUNK)
                sv = _dot(wsg, vn_ref[rows, g * gdim:(g + 1) * gdim]) + bg
                y_ref[rows, g * gdim:(g + 1) * gdim] = (
                    u2[rows, gg * gdim:(gg + 1) * gdim] * sv).astype(BF16)
    for r in range(2):
        rows = slice(r * (tb // 2), (r + 1) * (tb // 2))
        y = _dot(y_ref[rows, :], wo_ref[...])
        o_ref[rows, :] = _layer_norm(ALPHA * x_ref[rows, :] + y, g1_ref[...], b1_ref[...])


def _gmlp_layer(x2d, w_in, ln_g, ln_b, w_s, b_s, w_out, g1, b1, *, layer, tb):
    t, d = x2d.shape
    half = w_in.shape[1] // 2
    assert t % tb == 0 and tb % CHUNK == 0 and half % (GM_GROUPS * LANES) == 0
    row = lambda a: a.reshape(1, -1)
    tok = pl.BlockSpec((tb, d), lambda i: (i, 0))
    return pl.pallas_call(
        functools.partial(_gmlp_kernel, tb=tb, half=half),
        out_shape=jax.ShapeDtypeStruct((t, d), F32),
        grid=(t // tb,),
        in_specs=[tok, _resident((d, 2 * half)),
                  _resident((1, half)), _resident((1, half)),
                  _resident(w_s.shape), _resident((CHUNK, GM_GROUPS)),
                  _resident((half, d)), _resident((1, d), layer), _resident((1, d), layer)],
        out_specs=tok,
        scratch_shapes=[pltpu.VMEM((tb, half), BF16), pltpu.VMEM((tb, half), BF16)],
        compiler_params=_params(("arbitrary",)),
        name="gmlp_layer",
    )(x2d, w_in.astype(BF16), row(ln_g), row(ln_b), w_s, b_s.T, w_out.astype(BF16), g1, b1)


def _ffn_kernel(x_ref, p_ref, wg_ref, wu_ref, wd_ref, g2_ref, b2_ref,
                wpg_ref, bpg_ref, wpp_ref, o_ref):
    x = x_ref[...]
    xb = x.astype(BF16)
    g = _dot(xb, wg_ref[...])
    u = _dot(xb, wu_ref[...])
    h = (g * _sigmoid(g) * u).astype(BF16)
    y = _dot(h, wd_ref[...])
    x2 = _layer_norm(ALPHA * x + y, g2_ref[...], b2_ref[...])
    gate = _sigmoid(_dot(x2.astype(BF16), wpg_ref[...]) + bpg_ref[...])
    pp = _dot(p_ref[...].astype(BF16), wpp_ref[...])
    o_ref[...] = x2 + gate * pp


def _ffn_layer(x2d, p3d, w_gate, w_up, w_down, g2, b2, wpg, bpg, wpp, *, layer, tb):
    t, d = x2d.shape
    dff = w_gate.shape[-1]
    dp = p3d.shape[-1]
    assert t % tb == 0
    tok = pl.BlockSpec((tb, d), lambda i: (i, 0))
    return pl.pallas_call(
        _ffn_kernel,
        out_shape=jax.ShapeDtypeStruct((t, d), F32),
        grid=(t // tb,),
        in_specs=[tok, pl.BlockSpec((None, tb, dp), lambda i: (layer, i, 0)),
                  _resident((d, dff), layer), _resident((d, dff), layer), _resident((dff, d), layer),
                  _resident((1, d), layer), _resident((1, d), layer),
                  _resident((d, d), layer), _resident((1, d), layer), _resident((dp, d), layer)],
        out_specs=tok,
        compiler_params=_params(("arbitrary",)),
        name="ffn_layer",
    )(x2d, p3d, w_gate, w_up, w_down, g2, b2, wpg, bpg, wpp)


HALO = 8


def _dot_nt(a, b):
    return lax.dot_general(a, b, (((1,), (1,)), ((), ())), preferred_element_type=F32)


def _mlstm_proj_kernel(x_ref, wi_ref, cw_ref, cb_ref, wqk_ref, wkt_ref, wv_ref, wg_ref, bg_ref,
                       q_ref, kt_ref, v_ref, xc_ref, z_ref, gt_ref, ext_ref, *, tb, inner):
    xb = x_ref[...].astype(BF16)

    @pl.when(pl.program_id(1) == 0)
    def _():
        ext_ref[0:HALO, :] = jnp.zeros((HALO, inner), F32)

    xm = _dot(xb, wi_ref[:, :inner])
    ext_ref[HALO:HALO + tb, :] = xm
    z_ref[...] = _dot(xb, wi_ref[:, inner:]).astype(BF16)
    cw = cw_ref[...]
    conv = cw[ML_CONV - 1:ML_CONV, :] * xm + cb_ref[...]
    for j in range(1, ML_CONV):
        conv = conv + cw[ML_CONV - 1 - j:ML_CONV - j, :] * ext_ref[HALO - j:HALO - j + tb, :]
    ext_ref[0:HALO, :] = ext_ref[tb:tb + HALO, :]
    xc = conv * _sigmoid(conv)
    xcb = xc.astype(BF16)
    xmb = xm.astype(BF16)
    xc_ref[...] = xcb

    gates = jnp.broadcast_to(bg_ref[...], (tb, LANES))
    for g in range(inner // MXU_TILE):
        sl = slice(g * MXU_TILE, (g + 1) * MXU_TILE)
        qk = _dot(xcb[:, sl], wqk_ref[g])
        qg = qk[:, :MXU_TILE].astype(BF16)
        kg = qk[:, MXU_TILE:].astype(BF16)
        vg = _dot(xmb[:, sl], wv_ref[g]).astype(BF16)
        q_ref[:, sl] = qg
        kt_ref[sl, :] = _dot_nt(wkt_ref[g], xcb[:, sl]).astype(BF16)
        v_ref[:, sl] = vg
        gates = (gates + _dot(qg, wg_ref[sl, :])
                 + _dot(kg, wg_ref[inner + g * MXU_TILE:inner + (g + 1) * MXU_TILE, :])
                 + _dot(vg, wg_ref[2 * inner + g * MXU_TILE:2 * inner + (g + 1) * MXU_TILE, :]))
    gt_ref[...] = gates


def _block_diag_tiles(w, tile):
    nb, c, d = w.shape
    assert c == d and tile % c == 0 and nb % (tile // c) == 0
    cols = lax.broadcasted_iota(jnp.int32, (d, tile), 1)
    expand = (cols % d == lax.broadcasted_iota(jnp.int32, (d, tile), 0)).astype(F32)
    tiled = jnp.dot(w.reshape(nb * c, d), expand, precision=lax.Precision.HIGHEST)
    r = lax.broadcasted_iota(jnp.int32, (tile, tile), 0) // c
    s = lax.broadcasted_iota(jnp.int32, (tile, tile), 1) // d
    return jnp.where(r == s, tiled.reshape(nb * c // tile, tile, tile), 0.0).astype(BF16)


def _mlstm_proj(x2d, w_in, conv_w, conv_b, w_q, w_k, w_v, w_gates, b_gates, *, batch, tb):
    t, d = x2d.shape
    inner = w_in.shape[1] // 2
    seq = t // batch
    assert seq % tb == 0 and inner % MXU_TILE == 0 and tb >= HALO
    nblk = seq // tb
    ntile = inner // MXU_TILE
    wqk = jnp.concatenate([_block_diag_tiles(w_q, MXU_TILE), _block_diag_tiles(w_k, MXU_TILE)],
                          axis=-1)
    wkt = _block_diag_tiles(jnp.swapaxes(w_k, 1, 2), MXU_TILE)
    wv = _block_diag_tiles(w_v, MXU_TILE)
    ng = w_gates.shape[1]
    wg = jnp.pad(w_gates, ((0, 0), (0, LANES - ng))).astype(BF16)
    bg = jnp.pad(b_gates, (0, LANES - ng)).reshape(1, LANES)
    tok = lambda w: pl.BlockSpec((tb, w), lambda b, j: (b * nblk + j, 0))
    tok_t = pl.BlockSpec((None, inner, tb), lambda b, j: (b, 0, j))
    act = jax.ShapeDtypeStruct((t, inner), BF16)
    act_t = jax.ShapeDtypeStruct((batch, inner, seq), BF16)
    return pl.pallas_call(
        functools.partial(_mlstm_proj_kernel, tb=tb, inner=inner),
        out_shape=(act, act_t, act, act, act, jax.ShapeDtypeStruct((t, LANES), F32)),
        grid=(batch, nblk),
        in_specs=[tok(d), _resident((d, 2 * inner)),
                  _resident((ML_CONV, inner)), _resident((1, inner)),
                  _resident((ntile, MXU_TILE, 2 * MXU_TILE)), _resident((ntile, MXU_TILE, MXU_TILE)),
                  _resident((ntile, MXU_TILE, MXU_TILE)),
                  _resident((3 * inner, LANES)), _resident((1, LANES))],
        out_specs=(tok(inner), tok_t, tok(inner), tok(inner), tok(inner), tok(LANES)),
        scratch_shapes=[pltpu.VMEM((tb + HALO, inner), F32)],
        compiler_params=_params(("arbitrary", "arbitrary")),
        name="mlstm_proj",
    )(x2d, w_in.astype(BF16), conv_w, conv_b.reshape(1, inner), wqk, wkt, wv, wg, bg)


def _split3(a):
    hi = a.astype(BF16)
    r1 = a - hi.astype(F32)
    mid = r1.astype(BF16)
    lo = (r1 - mid.astype(F32)).astype(BF16)
    return hi, mid, lo


CELL_CHUNK = 256


def _mlstm_cell_kernel(q_ref, kt_ref, v_ref, xc_ref, z_ref, gt_ref, x_ref,
                       skip_ref, ng_ref, wo_ref, g1_ref, b1_ref, o_ref,
                       c_ref, n_ref, m_ref, hg_ref, *, nb, tb, dh, lc):
    nh = ML_HEADS
    scale = dh ** -0.5

    @pl.when(pl.program_id(1) == 0)
    def _():
        c_ref[...] = jnp.zeros(c_ref.shape, F32)
        n_ref[...] = jnp.zeros(n_ref.shape, F32)
        m_ref[...] = jnp.zeros(m_ref.shape, F32)

    row = lax.broadcasted_iota(jnp.int32, (lc, lc), 0)
    col = lax.broadcasted_iota(jnp.int32, (lc, lc), 1)
    causal = col <= row
    tri = jnp.where(causal, 1.0, 0.0).astype(BF16)
    lane = lax.broadcasted_iota(jnp.int32, (lc, LANES), 1)

    for c in range(tb // lc):
        rows = slice(c * lc, (c + 1) * lc)
        for b in range(nb):
            gt = gt_ref[b, rows, :]
            lf = jnp.minimum(gt, 0.0) - jnp.log(1.0 + jnp.exp(-jnp.abs(gt)))
            hi, mid, lo = _split3(lf)
            bcum = _dot(tri, hi) + _dot(tri, mid) + _dot(tri, lo)
            colmat = jnp.where(lane < nh, gt, bcum)
            rowmat = colmat.T
            heads = []
            for h in range(nh):
                hs = slice(h * dh, (h + 1) * dh)
                hd = dict(st=b * nh + h, hs=hs,
                          i_row=rowmat[h:h + 1, :], b_row=rowmat[nh + h:nh + h + 1, :],
                          i_col=colmat[:, h:h + 1], b_col=colmat[:, nh + h:nh + h + 1],
                          q=q_ref[b, rows, hs], kt=kt_ref[b, hs, rows], v=v_ref[b, rows, hs])
                hd["m_prev"] = m_ref[hd["st"]:hd["st"] + 1, 0:1]
                hd["qk"] = _dot(hd["q"], hd["kt"])
                heads.append(hd)

            for hd in heads:
                b_last = hd["b_row"][:, lc - 1:lc]
                g_row = b_last - hd["b_row"] + hd["i_row"]
                m_new = jnp.maximum(b_last + hd["m_prev"], jnp.max(g_row, axis=-1, keepdims=True))
                wg_row = jnp.exp(g_row - m_new)
                hd["decay"] = jnp.exp(b_last + hd["m_prev"] - m_new)
                hd["m_new"] = m_new
                kw = (hd["kt"].astype(F32) * wg_row).astype(BF16)
                hd["c_upd"] = _dot(kw, hd["v"])
                hd["n_upd"] = _dot_nt(jnp.ones((8, lc), BF16), kw)

            for hd in heads:
                st, hs = hd["st"], hd["hs"]
                dmat = jnp.where(causal, hd["b_col"] - hd["b_row"] + hd["i_row"], -jnp.inf)
                m_inter = hd["b_col"] + hd["m_prev"]
                m_t = jnp.maximum(m_inter, jnp.max(dmat, axis=-1, keepdims=True))
                s = hd["qk"] * scale * jnp.exp(dmat - m_t)
                si = jnp.exp(m_inter - m_t) * scale
                qs = hd["q"].astype(F32) * si
                num = _dot(s.astype(BF16), hd["v"]) + _dot(qs.astype(BF16), c_ref[st].astype(BF16))
                den = (jnp.sum(s, axis=-1, keepdims=True)
                       + jnp.sum(qs * n_ref[st], axis=-1, keepdims=True))
                hval = num / jnp.maximum(jnp.abs(den), jnp.exp(-m_t))
                z = z_ref[b, rows, hs].astype(F32)
                hg = ((_layer_norm(hval, ng_ref[:, hs])
                       + skip_ref[:, hs] * xc_ref[b, rows, hs].astype(F32)) * (z * _sigmoid(z)))
                hg_ref[b, rows, hs] = hg.astype(BF16)

            for hd in heads:
                st = hd["st"]
                c_ref[st] = hd["decay"] * c_ref[st] + hd["c_upd"]
                n_ref[st] = hd["decay"] * n_ref[st] + hd["n_upd"][0:1, :]
                m_ref[st:st + 1, :] = jnp.broadcast_to(hd["m_new"], (1, LANES))

            y = _dot(hg_ref[b, rows, :], wo_ref[...])
            o_ref[b, rows, :] = _layer_norm(ALPHA * x_ref[b, rows, :] + y, g1_ref[...], b1_ref[...])


def _mlstm_cell_layer(q, kt, v, xc, z, gt, x3d, skip, norm_g, w_out, g1, b1, *, layer, nb, tb):
    batch, seq, inner = q.shape
    d = x3d.shape[-1]
    dh = inner // ML_HEADS
    assert seq % tb == 0 and tb % CELL_CHUNK == 0 and batch % nb == 0
    row = lambda a: a.reshape(1, -1)
    tok = lambda w: pl.BlockSpec((nb, tb, w), lambda bi, j: (bi, j, 0))
    tok_t = pl.BlockSpec((nb, inner, tb), lambda bi, j: (bi, 0, j))
    nstate = nb * ML_HEADS
    return pl.pallas_call(
        functools.partial(_mlstm_cell_kernel, nb=nb, tb=tb, dh=dh, lc=CELL_CHUNK),
        out_shape=jax.ShapeDtypeStruct((batch, seq, d), F32),
        grid=(batch // nb, seq // tb),
        in_specs=[tok(inner), tok_t, tok(inner), tok(inner), tok(inner), tok(LANES), tok(d),
                  _resident((1, inner)), _resident((1, inner)), _resident((inner, d)),
                  _resident((1, d), layer), _resident((1, d), layer)],
        out_specs=tok(d),
        scratch_shapes=[pltpu.VMEM((nstate, dh, dh), F32),
                        pltpu.VMEM((nstate, 1, dh), F32),
                        pltpu.VMEM((-(-nstate // 8) * 8, LANES), F32),
                        pltpu.VMEM((nb, tb, inner), BF16)],
        compiler_params=_params(("arbitrary", "arbitrary")),
        name="mlstm_cell",
    )(q, kt, v, xc, z, gt, x3d, row(skip), row(norm_g), w_out.astype(BF16), g1, b1)


def kernel(x, p, gm_w_in, gm_ln_g, gm_ln_b, gm_w_s, gm_b_s, gm_w_out, ml_w_in, ml_conv_w, ml_conv_b, ml_w_q, ml_w_k, ml_w_v, ml_w_gates, ml_b_gates, ml_skip, ml_norm_g, ml_w_out, ln1_g, ln1_b, ln2_g, ln2_b, ffn_w_gate, ffn_w_up, ffn_w_down, ple_w_proj, ple_w_gate, ple_b_gate):
    batch, seq, d = x.shape
    t = batch * seq
    xs = x.reshape(t, d)
    ps = p.reshape(p.shape[0], t, p.shape[-1])
    vec = lambda a: a.reshape(a.shape[0], 1, a.shape[-1])
    ln1g, ln1b, ln2g, ln2b, pbg = vec(ln1_g), vec(ln1_b), vec(ln2_g), vec(ln2_b), vec(ple_b_gate)
    ffn_w = [w.astype(BF16) for w in (ffn_w_gate, ffn_w_up, ffn_w_down)]
    wpg, wpp = ple_w_gate.astype(BF16), ple_w_proj.astype(BF16)

    def ffn(xs, i):
        return _ffn_layer(xs, ps, *ffn_w, ln2g, ln2b, wpg, pbg, wpp, layer=i, tb=512)

    xs = _gmlp_layer(xs, gm_w_in[0], gm_ln_g[0], gm_ln_b[0], gm_w_s[0], gm_b_s[0], gm_w_out[0],
                     ln1g, ln1b, layer=0, tb=512)
    xs = ffn(xs, 0)
    q, kt, v, xc, z, gt = _mlstm_proj(xs, ml_w_in[0], ml_conv_w[0], ml_conv_b[0], ml_w_q[0],
                                      ml_w_k[0], ml_w_v[0], ml_w_gates[0], ml_b_gates[0],
                                      batch=batch, tb=512)
    seq3 = lambda a: a.reshape(batch, seq, a.shape[-1])
    xs = _mlstm_cell_layer(seq3(q), kt, seq3(v), seq3(xc), seq3(z), seq3(gt), seq3(xs),
                           ml_skip[0], ml_norm_g[0], ml_w_out[0], ln1g, ln1b, layer=1, nb=1, tb=256)
    xs = ffn(xs.reshape(t, d), 1)
    return xs.reshape(batch, seq, d)
```

```python
import functools
import math

import jax
import jax.numpy as jnp
from jax import lax
from jax.experimental import pallas as pl
from jax.experimental.pallas import tpu as pltpu

F32 = jnp.float32
BF16 = jnp.bfloat16

CHUNK = 128
LN_EPS = 1e-5
DEPTH = 2
ALPHA = (2 * DEPTH) ** 0.25
GM_GROUPS = 8
ML_HEADS = 4
ML_CONV = 4
ML_QKV_BLOCK = 4

LANES = 128
MXU_TILE = 256
VMEM_LIMIT_BYTES = 56 * 1024 * 1024


def _dot(a, b):
    return jnp.dot(a, b, preferred_element_type=F32)


def _sigmoid(x):
    return 1.0 / (1.0 + jnp.exp(-x))


def _gelu_tanh(x):
    c = 0.7978845608028654
    hx = 0.5 * x
    return hx + hx * jnp.tanh(x * (c + (c * 0.044715) * (x * x)))


def _layer_norm(x, g, b=None):
    mu = jnp.mean(x, axis=-1, keepdims=True)
    xc = x - mu
    var = jnp.mean(xc * xc, axis=-1, keepdims=True)
    y = xc * lax.rsqrt(var + LN_EPS) * g
    if b is not None:
        y = y + b
    return y


def _resident(shape, layer=None):
    nd = len(shape)
    if layer is None:
        return pl.BlockSpec(shape, lambda *_: (0,) * nd, pipeline_mode=pl.Buffered(1))
    return pl.BlockSpec((None,) + tuple(shape), lambda *_: (layer,) + (0,) * nd,
                        pipeline_mode=pl.Buffered(1))


def _params(semantics):
    return pltpu.CompilerParams(dimension_semantics=semantics,
                                vmem_limit_bytes=VMEM_LIMIT_BYTES)


def _gmlp_kernel(x_ref, wi_ref, lng_ref, lnb_ref, ws_ref, bs_ref, wo_ref,
                 g1_ref, b1_ref, o_ref, vn_ref, y_ref, *, tb, half):
    gdim = half // GM_GROUPS
    x = x_ref[...]
    xb = x.astype(BF16)
    pair = 2 * gdim
    u_pair = lambda j: _gelu_tanh(_dot(xb, wi_ref[:, j * pair:(j + 1) * pair]))
    v = _gelu_tanh(_dot(xb, wi_ref[:, half:]))
    u_next = u_pair(0)
    vn_ref[...] = _layer_norm(v, lng_ref[...], lnb_ref[...]).astype(BF16)

    row = lax.broadcasted_iota(jnp.int32, (CHUNK, CHUNK), 0)
    col = lax.broadcasted_iota(jnp.int32, (CHUNK, CHUNK), 1)
    causal = col <= row
    bs = bs_ref[...]
    for j in range(GM_GROUPS // 2):
        u2 = u_next if j == 0 else u_pair(j)
        for gg in range(2):
            g = 2 * j + gg
            wsg = jnp.where(causal, ws_ref[g], 0.0).astype(BF16)
            bg = bs[:, g:g + 1]
            for c in range(tb // CHUNK):
                rows = slice(c * CHUNK, (c + 1) * CHUNK)
                sv = _dot(wsg, vn_ref[rows, g * gdim:(g + 1) * gdim]) + bg
                y_ref[rows, g * gdim:(g + 1) * gdim] = (
                    u2[rows, gg * gdim:(gg + 1) * gdim] * sv).astype(BF16)
    for r in range(2):
        rows = slice(r * (tb // 2), (r + 1) * (tb // 2))
        y = _dot(y_ref[rows, :], wo_ref[...])
        o_ref[rows, :] = _layer_norm(ALPHA * x_ref[rows, :] + y, g1_ref[...], b1_ref[...])


def _gmlp_layer(x2d, w_in, ln_g, ln_b, w_s, b_s, w_out, g1, b1, *, layer, tb):
    t, d = x2d.shape
    half = w_in.shape[1] // 2
    assert t % tb == 0 and tb % CHUNK == 0 and half % (GM_GROUPS * LANES) == 0
    row = lambda a: a.reshape(1, -1)
    tok = pl.BlockSpec((tb, d), lambda i: (i, 0))
    return pl.pallas_call(
        functools.partial(_gmlp_kernel, tb=tb, half=half),
        out_shape=jax.ShapeDtypeStruct((t, d), F32),
        grid=(t // tb,),
        in_specs=[tok, _resident((d, 2 * half)),
                  _resident((1, half)), _resident((1, half)),
                  _resident(w_s.shape), _resident((CHUNK, GM_GROUPS)),
                  _resident((half, d)), _resident((1, d), layer), _resident((1, d), layer)],
        out_specs=tok,
        scratch_shapes=[pltpu.VMEM((tb, half), BF16), pltpu.VMEM((tb, half), BF16)],
        compiler_params=_params(("arbitrary",)),
        name="gmlp_layer",
    )(x2d, w_in.astype(BF16), row(ln_g), row(ln_b), w_s, b_s.T, w_out.astype(BF16), g1, b1)


def _ffn_kernel(x_ref, p_ref, wg_ref, wu_ref, wd_ref, g2_ref, b2_ref,
                wpg_ref, bpg_ref, wpp_ref, o_ref):
    x = x_ref[...]
    xb = x.astype(BF16)
    g = _dot(xb, wg_ref[...])
    u = _dot(xb, wu_ref[...])
    h = (g * _sigmoid(g) * u).astype(BF16)
    y = _dot(h, wd_ref[...])
    x2 = _layer_norm(ALPHA * x + y, g2_ref[...], b2_ref[...])
    gate = _sigmoid(_dot(x2.astype(BF16), wpg_ref[...]) + bpg_ref[...])
    pp = _dot(p_ref[...].astype(BF16), wpp_ref[...])
    o_ref[...] = x2 + gate * pp


def _ffn_layer(x2d, p3d, w_gate, w_up, w_down, g2, b2, wpg, bpg, wpp, *, layer, tb):
    t, d = x2d.shape
    dff = w_gate.shape[-1]
    dp = p3d.shape[-1]
    assert t % tb == 0
    tok = pl.BlockSpec((tb, d), lambda i: (i, 0))
    return pl.pallas_call(
        _ffn_kernel,
        out_shape=jax.ShapeDtypeStruct((t, d), F32),
        grid=(t // tb,),
        in_specs=[tok, pl.BlockSpec((None, tb, dp), lambda i: (layer, i, 0)),
                  _resident((d, dff), layer), _resident((d, dff), layer), _resident((dff, d), layer),
                  _resident((1, d), layer), _resident((1, d), layer),
                  _resident((d, d), layer), _resident((1, d), layer), _resident((dp, d), layer)],
        out_specs=tok,
        compiler_params=_params(("arbitrary",)),
        name="ffn_layer",
    )(x2d, p3d, w_gate, w_up, w_down, g2, b2, wpg, bpg, wpp)


HALO = 8


def _dot_nt(a, b):
    return lax.dot_general(a, b, (((1,), (1,)), ((), ())), preferred_element_type=F32)


def _mlstm_proj_kernel(x_ref, wi_ref, cw_ref, cb_ref, wqk_ref, wkt_ref, wv_ref, wg_ref, bg_ref,
                       skip_ref, ng_ref, q_ref, kt_ref, v_ref, cz_ref, gz_ref, gt_ref, ext_ref,
                       *, tb, inner):
    xb = x_ref[...].astype(BF16)

    @pl.when(pl.program_id(1) == 0)
    def _():
        ext_ref[0:HALO, :] = jnp.zeros((HALO, inner), F32)

    xm = _dot(xb, wi_ref[:, :inner])
    ext_ref[HALO:HALO + tb, :] = xm
    z = _dot(xb, wi_ref[:, inner:])
    sz = z * _sigmoid(z)
    gz_ref[...] = (ng_ref[...] * sz).astype(BF16)
    cw = cw_ref[...]
    conv = cw[ML_CONV - 1:ML_CONV, :] * xm + cb_ref[...]
    for j in range(1, ML_CONV):
        conv = conv + cw[ML_CONV - 1 - j:ML_CONV - j, :] * ext_ref[HALO - j:HALO - j + tb, :]
    ext_ref[0:HALO, :] = ext_ref[tb:tb + HALO, :]
    xc = conv * _sigmoid(conv)
    xcb = xc.astype(BF16)
    xmb = xm.astype(BF16)
    cz_ref[...] = (skip_ref[...] * xc * sz).astype(BF16)

    gates = jnp.broadcast_to(bg_ref[...], (tb, LANES))
    for g in range(inner // MXU_TILE):
        sl = slice(g * MXU_TILE, (g + 1) * MXU_TILE)
        qk = _dot(xcb[:, sl], wqk_ref[g])
        qg = qk[:, :MXU_TILE].astype(BF16)
        kg = qk[:, MXU_TILE:].astype(BF16)
        vg = _dot(xmb[:, sl], wv_ref[g]).astype(BF16)
        q_ref[:, sl] = qg
        kt_ref[sl, :] = _dot_nt(wkt_ref[g], xcb[:, sl]).astype(BF16)
        v_ref[:, sl] = vg
        gates = (gates + _dot(qg, wg_ref[sl, :])
                 + _dot(kg, wg_ref[inner + g * MXU_TILE:inner + (g + 1) * MXU_TILE, :])
                 + _dot(vg, wg_ref[2 * inner + g * MXU_TILE:2 * inner + (g + 1) * MXU_TILE, :]))
    gt_ref[...] = gates


def _block_diag_tiles(w, tile):
    nb, c, d = w.shape
    assert c == d and tile % c == 0 and nb % (tile // c) == 0
    cols = lax.broadcasted_iota(jnp.int32, (d, tile), 1)
    expand = (cols % d == lax.broadcasted_iota(jnp.int32, (d, tile), 0)).astype(F32)
    tiled = jnp.dot(w.reshape(nb * c, d), expand, precision=lax.Precision.HIGHEST)
    r = lax.broadcasted_iota(jnp.int32, (tile, tile), 0) // c
    s = lax.broadcasted_iota(jnp.int32, (tile, tile), 1) // d
    return jnp.where(r == s, tiled.reshape(nb * c // tile, tile, tile), 0.0).astype(BF16)


def _mlstm_proj(x2d, w_in, conv_w, conv_b, w_q, w_k, w_v, w_gates, b_gates, skip, norm_g,
                *, batch, tb):
    t, d = x2d.shape
    inner = w_in.shape[1] // 2
    seq = t // batch
    assert seq % tb == 0 and inner % MXU_TILE == 0 and tb >= HALO
    nblk = seq // tb
    ntile = inner // MXU_TILE
    wqk = jnp.concatenate([_block_diag_tiles(w_q, MXU_TILE), _block_diag_tiles(w_k, MXU_TILE)],
                          axis=-1)
    wkt = _block_diag_tiles(jnp.swapaxes(w_k, 1, 2), MXU_TILE)
    wv = _block_diag_tiles(w_v, MXU_TILE)
    ng = w_gates.shape[1]
    wg = jnp.pad(w_gates, ((0, 0), (0, LANES - ng))).astype(BF16)
    bg = jnp.pad(b_gates, (0, LANES - ng)).reshape(1, LANES)
    tok = lambda w: pl.BlockSpec((tb, w), lambda b, j: (b * nblk + j, 0))
    tok_t = pl.BlockSpec((None, inner, tb), lambda b, j: (b, 0, j))
    act = jax.ShapeDtypeStruct((t, inner), BF16)
    act_t = jax.ShapeDtypeStruct((batch, inner, seq), BF16)
    return pl.pallas_call(
        functools.partial(_mlstm_proj_kernel, tb=tb, inner=inner),
        out_shape=(act, act_t, act, act, act, jax.ShapeDtypeStruct((t, LANES), F32)),
        grid=(batch, nblk),
        in_specs=[tok(d), _resident((d, 2 * inner)),
                  _resident((ML_CONV, inner)), _resident((1, inner)),
                  _resident((ntile, MXU_TILE, 2 * MXU_TILE)), _resident((ntile, MXU_TILE, MXU_TILE)),
                  _resident((ntile, MXU_TILE, MXU_TILE)),
                  _resident((3 * inner, LANES)), _resident((1, LANES)),
                  _resident((1, inner)), _resident((1, inner))],
        out_specs=(tok(inner), tok_t, tok(inner), tok(inner), tok(inner), tok(LANES)),
        scratch_shapes=[pltpu.VMEM((tb + HALO, inner), F32)],
        compiler_params=_params(("arbitrary", "arbitrary")),
        name="mlstm_proj",
    )(x2d, w_in.astype(BF16), conv_w, conv_b.reshape(1, inner), wqk, wkt, wv, wg, bg,
      skip.reshape(1, inner), norm_g.reshape(1, inner))


def _split3(a):
    hi = a.astype(BF16)
    r1 = a - hi.astype(F32)
    mid = r1.astype(BF16)
    lo = (r1 - mid.astype(F32)).astype(BF16)
    return hi, mid, lo


CELL_CHUNK = 256


def _mlstm_cell_kernel(q_ref, kt_ref, v_ref, cz_ref, gz_ref, gt_ref, x_ref,
                       wo_ref, g1_ref, b1_ref, o_ref,
                       c_ref, n_ref, m_ref, hg_ref, cb_ref, *, nb, tb, dh, lc):
    nh = ML_HEADS
    scale = dh ** -0.5
    log_scale = -0.5 * math.log(dh)

    @pl.when(pl.program_id(1) == 0)
    def _():
        c_ref[...] = jnp.zeros(c_ref.shape, F32)
        n_ref[...] = jnp.zeros(n_ref.shape, F32)
        m_ref[...] = jnp.zeros(m_ref.shape, F32)

    row = lax.broadcasted_iota(jnp.int32, (lc, lc), 0)
    col = lax.broadcasted_iota(jnp.int32, (lc, lc), 1)
    causal = col <= row
    tri = jnp.where(causal, 1.0, 0.0).astype(BF16)
    lane = lax.broadcasted_iota(jnp.int32, (lc, LANES), 1)

    for c in range(tb // lc):
        rows = slice(c * lc, (c + 1) * lc)
        for b in range(nb):
            gt = gt_ref[b, rows, :]
            lf = jnp.minimum(gt, 0.0) - jnp.log(1.0 + jnp.exp(-jnp.abs(gt)))
            hi, mid, lo = _split3(lf)
            bcum = _dot(tri, hi) + _dot(tri, mid) + _dot(tri, lo)
            colmat = jnp.where(lane < nh, gt, bcum)
            rowmat = colmat.T

            def early(h):
                hs = slice(h * dh, (h + 1) * dh)
                hd = dict(st=b * nh + h, hs=hs,
                          i_row=rowmat[h:h + 1, :], b_row=rowmat[nh + h:nh + h + 1, :],
                          b_col=colmat[:, nh + h:nh + h + 1],
                          q=q_ref[b, rows, hs], kt=kt_ref[b, hs, rows], v=v_ref[b, rows, hs])
                hd["m_prev"] = m_ref[hd["st"]:hd["st"] + 1, 0:1]
                hd["n_prev"] = n_ref[hd["st"]]
                cb_ref[hd["st"]] = c_ref[hd["st"]].astype(BF16)
                hd["qk"] = _dot(hd["q"], hd["kt"])
                return hd

            heads = [early(h) for h in range(nh)]

            def update(hd):
                b_last = hd["b_row"][:, lc - 1:lc]
                g_row = b_last - hd["b_row"] + hd["i_row"]
                m_new = jnp.maximum(b_last + hd["m_prev"], jnp.max(g_row, axis=-1, keepdims=True))
                wg_row = jnp.exp(g_row - m_new)
                hd["decay"] = jnp.exp(b_last + hd["m_prev"] - m_new)
                hd["m_new"] = m_new
                kw = (hd["kt"].astype(F32) * wg_row).astype(BF16)
                hd["c_upd"] = _dot(kw, hd["v"])
                hd["n_upd"] = _dot_nt(jnp.ones((8, lc), BF16), kw)

            def output(hd):
                st, hs = hd["st"], hd["hs"]
                dmat = jnp.where(causal, hd["b_col"] - hd["b_row"] + hd["i_row"], -jnp.inf)
                m_inter = hd["b_col"] + hd["m_prev"]
                m_t = jnp.maximum(m_inter, jnp.max(dmat, axis=-1, keepdims=True))
                s = hd["qk"] * jnp.exp(dmat - (m_t - log_scale))
                si = jnp.exp(m_inter - m_t) * scale
                qs = hd["q"].astype(F32) * si
                num = _dot(s.astype(BF16), hd["v"]) + _dot(qs.astype(BF16), cb_ref[st])
                den = (jnp.sum(s, axis=-1, keepdims=True)
                       + jnp.sum(qs * hd["n_prev"], axis=-1, keepdims=True))
                r = 1.0 / jnp.maximum(jnp.abs(den), jnp.exp(-m_t))
                nc = num - jnp.mean(num, axis=-1, keepdims=True)
                var = jnp.mean(nc * nc, axis=-1, keepdims=True)
                hn = nc * (r * lax.rsqrt(r * r * var + LN_EPS))
                hg = hn * gz_ref[b, rows, hs].astype(F32) + cz_ref[b, rows, hs].astype(F32)
                hg_ref[b, rows, hs] = hg.astype(BF16)

            def advance(hd):
                st = hd["st"]
                c_ref[st] = hd["decay"] * c_ref[st] + hd["c_upd"]
                n_ref[st] = hd["decay"] * hd["n_prev"] + hd["n_upd"][0:1, :]
                m_ref[st:st + 1, :] = jnp.broadcast_to(hd["m_new"], (1, LANES))

            for hd in heads:
                update(hd)
            for hd in heads:
                advance(hd)
            for hd in heads:
                output(hd)

            y = _dot(hg_ref[b, rows, :], wo_ref[...])
            o_ref[b, rows, :] = _layer_norm(ALPHA * x_ref[b, rows, :] + y, g1_ref[...], b1_ref[...])


def _mlstm_cell_layer(q, kt, v, cz, gz, gt, x3d, w_out, g1, b1, *, layer, nb, tb):
    batch, seq, inner = q.shape
    d = x3d.shape[-1]
    dh = inner // ML_HEADS
    assert seq % tb == 0 and tb % CELL_CHUNK == 0 and batch % nb == 0
    tok = lambda w: pl.BlockSpec((nb, tb, w), lambda bi, j: (bi, j, 0))
    tok_t = pl.BlockSpec((nb, inner, tb), lambda bi, j: (bi, 0, j))
    nstate = nb * ML_HEADS
    return pl.pallas_call(
        functools.partial(_mlstm_cell_kernel, nb=nb, tb=tb, dh=dh, lc=CELL_CHUNK),
        out_shape=jax.ShapeDtypeStruct((batch, seq, d), F32),
        grid=(batch // nb, seq // tb),
        in_specs=[tok(inner), tok_t, tok(inner), tok(inner), tok(inner), tok(LANES), tok(d),
                  _resident((inner, d)), _resident((1, d), layer), _resident((1, d), layer)],
        out_specs=tok(d),
        scratch_shapes=[pltpu.VMEM((nstate, dh, dh), F32),
                        pltpu.VMEM((nstate, 1, dh), F32),
                        pltpu.VMEM((-(-nstate // 8) * 8, LANES), F32),
                        pltpu.VMEM((nb, tb, inner), BF16),
                        pltpu.VMEM((nstate, dh, dh), BF16)],
        compiler_params=_params(("arbitrary", "arbitrary")),
        name="mlstm_cell",
    )(q, kt, v, cz, gz, gt, x3d, w_out.astype(BF16), g1, b1)


def kernel(x, p, gm_w_in, gm_ln_g, gm_ln_b, gm_w_s, gm_b_s, gm_w_out, ml_w_in, ml_conv_w, ml_conv_b, ml_w_q, ml_w_k, ml_w_v, ml_w_gates, ml_b_gates, ml_skip, ml_norm_g, ml_w_out, ln1_g, ln1_b, ln2_g, ln2_b, ffn_w_gate, ffn_w_up, ffn_w_down, ple_w_proj, ple_w_gate, ple_b_gate):
    batch, seq, d = x.shape
    t = batch * seq
    xs = x.reshape(t, d)
    ps = p.reshape(p.shape[0], t, p.shape[-1])
    vec = lambda a: a.reshape(a.shape[0], 1, a.shape[-1])
    ln1g, ln1b, ln2g, ln2b, pbg = vec(ln1_g), vec(ln1_b), vec(ln2_g), vec(ln2_b), vec(ple_b_gate)
    ffn_w = [w.astype(BF16) for w in (ffn_w_gate, ffn_w_up, ffn_w_down)]
    wpg, wpp = ple_w_gate.astype(BF16), ple_w_proj.astype(BF16)

    def ffn(xs, i):
        return _ffn_layer(xs, ps, *ffn_w, ln2g, ln2b, wpg, pbg, wpp, layer=i, tb=512)

    xs = _gmlp_layer(xs, gm_w_in[0], gm_ln_g[0], gm_ln_b[0], gm_w_s[0], gm_b_s[0], gm_w_out[0],
                     ln1g, ln1b, layer=0, tb=512)
    xs = ffn(xs, 0)
    q, kt, v, cz, gz, gt = _mlstm_proj(xs, ml_w_in[0], ml_conv_w[0], ml_conv_b[0], ml_w_q[0],
                                       ml_w_k[0], ml_w_v[0], ml_w_gates[0], ml_b_gates[0],
                                       ml_skip[0], ml_norm_g[0], batch=batch, tb=512)
    seq3 = lambda a: a.reshape(batch, seq, a.shape[-1])
    xs = _mlstm_cell_layer(seq3(q), kt, seq3(v), seq3(cz), seq3(gz), seq3(gt), seq3(xs),
                           ml_w_out[0], ln1g, ln1b, layer=1, nb=1, tb=256)
    xs = ffn(xs.reshape(t, d), 1)
    return xs.reshape(batch, seq, d)
```

```python
import functools
import math
from typing import NamedTuple

import jax
import jax.numpy as jnp
from jax import lax
from jax.experimental import pallas as pl
from jax.experimental.pallas import tpu as pltpu

F32 = jnp.float32
BF16 = jnp.bfloat16

CHUNK = 128
LN_EPS = 1e-5
DEPTH = 2
ALPHA = (2 * DEPTH) ** 0.25
GM_GROUPS = 8
ML_HEADS = 4
ML_CONV = 4
ML_QKV_BLOCK = 4

LANES = 128
MXU_TILE = 256
VMEM_LIMIT_BYTES = 56 * 1024 * 1024


def _dot(a, b):
    return jnp.dot(a, b, preferred_element_type=F32)


def _sigmoid(x):
    return 1.0 / (1.0 + jnp.exp(-x))


def _gelu_tanh(x):
    c = 0.7978845608028654
    hx = 0.5 * x
    return hx + hx * jnp.tanh(x * (c + (c * 0.044715) * (x * x)))


def _layer_norm(x, g, b, eps=LN_EPS):
    mu = jnp.mean(x, axis=-1, keepdims=True)
    xc = x - mu
    var = jnp.mean(xc * xc, axis=-1, keepdims=True)
    return xc * lax.rsqrt(var + eps) * g + b


def _deepnorm(x, y_over_alpha, g, b):
    return _layer_norm(x + y_over_alpha, g, b, eps=LN_EPS / ALPHA ** 2)


def _layer_norm_one_pass(x, g, b):
    inv_n = 1.0 / x.shape[-1]
    mu = jnp.sum(x, axis=-1, keepdims=True) * inv_n
    var = jnp.sum(x * x, axis=-1, keepdims=True) * inv_n - mu * mu
    return (x - mu) * lax.rsqrt(var + LN_EPS) * g + b


def _resident(shape, layer=None):
    nd = len(shape)
    if layer is None:
        return pl.BlockSpec(shape, lambda *_: (0,) * nd, pipeline_mode=pl.Buffered(1))
    return pl.BlockSpec((None,) + tuple(shape), lambda *_: (layer,) + (0,) * nd,
                        pipeline_mode=pl.Buffered(1))


class _SideCast(NamedTuple):
    w: jax.Array
    layer: int | None = None
    scale: float = 1.0


BF16_SUBLANES = 16


def _side_cast_specs(casts, steps, step_of):
    in_specs, out_specs, out_shapes = [], [], []
    for c in casts:
        k, n = c.w.shape[-2:]
        cuts = [(k // r, n // cc) for r in range(BF16_SUBLANES, k + 1, BF16_SUBLANES) if k % r == 0
                for cc in range(LANES, n + 1, LANES) if n % cc == 0
                if steps % ((k // r) * (n // cc)) == 0]
        nr, ncol = max(cuts, key=lambda rc: (rc[0] * rc[1], -rc[1]))
        per = steps // (nr * ncol)

        def cell(*g, per=per, ncol=ncol):
            i = step_of(*g) // per
            return i // ncol, i % ncol

        block = (k // nr, n // ncol)
        if c.layer is None:
            in_specs.append(pl.BlockSpec(block, cell))
        else:
            in_specs.append(pl.BlockSpec((None,) + block, lambda *g, cell=cell, l=c.layer: (l, *cell(*g))))
        out_specs.append(pl.BlockSpec(block, cell))
        out_shapes.append(jax.ShapeDtypeStruct((k, n), BF16))
    return in_specs, out_specs, out_shapes


def _with_side_casts(body, n_in, n_out, casts):
    nc = len(casts)

    def kernel(*refs):
        cast_in = refs[n_in:n_in + nc]
        cast_out = refs[n_in + nc + n_out:n_in + 2 * nc + n_out]
        for c, src, dst in zip(casts, cast_in, cast_out):
            w = src[...]
            dst[...] = (w if c.scale == 1.0 else w * c.scale).astype(BF16)
        body(*refs[:n_in], *refs[n_in + nc:n_in + nc + n_out], *refs[n_in + 2 * nc + n_out:])

    return kernel


def _params(semantics):
    return pltpu.CompilerParams(dimension_semantics=semantics,
                                vmem_limit_bytes=VMEM_LIMIT_BYTES)


def _gmlp_kernel(x_ref, wi_ref, lng_ref, lnb_ref, ws_ref, bs_ref, wo_ref,
                 g1_ref, b1_ref, o_ref, vn_ref, y_ref, *, tb, half):
    gdim = half // GM_GROUPS
    x = x_ref[...]
    xb = x.astype(BF16)
    pair = 2 * gdim
    u_pair = lambda j: _gelu_tanh(_dot(xb, wi_ref[:, j * pair:(j + 1) * pair]))
    v = _gelu_tanh(_dot(xb, wi_ref[:, half:]))
    u_next = u_pair(0)
    vn_ref[...] = _layer_norm_one_pass(v, lng_ref[...], lnb_ref[...]).astype(BF16)

    row = lax.broadcasted_iota(jnp.int32, (CHUNK, CHUNK), 0)
    col = lax.broadcasted_iota(jnp.int32, (CHUNK, CHUNK), 1)
    causal = col <= row
    bs = bs_ref[...]
    for j in range(GM_GROUPS // 2):
        u2 = u_next if j == 0 else u_pair(j)
        for gg in range(2):
            g = 2 * j + gg
            wsg = jnp.where(causal, ws_ref[g], 0.0).astype(BF16)
            bg = bs[:, g:g + 1]
            for c in range(tb // CHUNK):
                rows = slice(c * CHUNK, (c + 1) * CHUNK)
                sv = _dot(wsg, vn_ref[rows, g * gdim:(g + 1) * gdim]) + bg
                y_ref[rows, g * gdim:(g + 1) * gdim] = (
                    u2[rows, gg * gdim:(gg + 1) * gdim] * sv).astype(BF16)
    for r in range(2):
        rows = slice(r * (tb // 2), (r + 1) * (tb // 2))
        y = _dot(y_ref[rows, :], wo_ref[...])
        o_ref[rows, :] = _deepnorm(x_ref[rows, :], y, g1_ref[...], b1_ref[...])


def _gmlp_layer(x2d, w_in, ln_g, ln_b, w_s, b_s, w_out, g1, b1, *, layer, tb, casts=()):
    t, d = x2d.shape
    half = w_in.shape[1] // 2
    assert t % tb == 0 and tb % CHUNK == 0 and half % (GM_GROUPS * LANES) == 0
    row = lambda a: a.reshape(1, -1)
    tok = pl.BlockSpec((tb, d), lambda i: (i, 0))
    c_in, c_out, c_shapes = _side_cast_specs(casts, t // tb, lambda i: i)
    in_specs = [tok, _resident((d, 2 * half)),
                _resident((1, half)), _resident((1, half)),
                _resident(w_s.shape), _resident((CHUNK, GM_GROUPS)),
                _resident((half, d)), _resident((1, d), layer), _resident((1, d), layer)]
    return pl.pallas_call(
        _with_side_casts(functools.partial(_gmlp_kernel, tb=tb, half=half), len(in_specs), 1, casts),
        out_shape=(jax.ShapeDtypeStruct((t, d), F32), *c_shapes),
        grid=(t // tb,),
        in_specs=in_specs + c_in,
        out_specs=(tok, *c_out),
        scratch_shapes=[pltpu.VMEM((tb, half), BF16), pltpu.VMEM((tb, half), BF16)],
        compiler_params=_params(("arbitrary",)),
        name="gmlp_layer",
    )(x2d, w_in.astype(BF16), row(ln_g), row(ln_b), w_s, b_s.T,
      (w_out * (1.0 / ALPHA)).astype(BF16), g1, b1, *[c.w for c in casts])


def _ffn_kernel(x_ref, p_ref, wg_ref, wu_ref, wd_ref, g2_ref, b2_ref,
                wpg_ref, bpg_ref, wpp_ref, o_ref):
    x = x_ref[...]
    xb = x.astype(BF16)
    g = _dot(xb, wg_ref[...])
    u = _dot(xb, wu_ref[...])
    h = (g * _sigmoid(g) * u).astype(BF16)
    y = _dot(h, wd_ref[...])
    x2 = _deepnorm(x, y, g2_ref[...], b2_ref[...])
    gate = _sigmoid(_dot(x2.astype(BF16), wpg_ref[...]) + bpg_ref[...])
    pp = _dot(p_ref[...].astype(BF16), wpp_ref[...])
    o_ref[...] = x2 + gate * pp


def _ffn_layer(x2d, p3d, w_gate, w_up, w_down, g2, b2, wpg, bpg, wpp, *, layer, tb, casts=()):
    t, d = x2d.shape
    dff = w_gate.shape[-1]
    dp = p3d.shape[-1]
    assert t % tb == 0
    tok = pl.BlockSpec((tb, d), lambda i: (i, 0))
    c_in, c_out, c_shapes = _side_cast_specs(casts, t // tb, lambda i: i)
    in_specs = [tok, pl.BlockSpec((None, tb, dp), lambda i: (layer, i, 0)),
                _resident((d, dff)), _resident((d, dff)), _resident((dff, d)),
                _resident((1, d), layer), _resident((1, d), layer),
                _resident((d, d)), _resident((1, d), layer), _resident((dp, d))]
    return pl.pallas_call(
        _with_side_casts(_ffn_kernel, len(in_specs), 1, casts),
        out_shape=(jax.ShapeDtypeStruct((t, d), F32), *c_shapes),
        grid=(t // tb,),
        in_specs=in_specs + c_in,
        out_specs=(tok, *c_out),
        compiler_params=_params(("arbitrary",)),
        name="ffn_layer",
    )(x2d, p3d, w_gate, w_up, w_down, g2, b2, wpg, bpg, wpp, *[c.w for c in casts])


HALO = 8


def _dot_nt(a, b):
    return lax.dot_general(a, b, (((1,), (1,)), ((), ())), preferred_element_type=F32)


def _mlstm_proj_kernel(x_ref, wi_ref, cw_ref, cb_ref, wqk_ref, wkt_ref, wv_ref, wg_ref, bg_ref,
                       skip_ref, ng_ref, q_ref, kt_ref, v_ref, cz_ref, gz_ref, gt_ref, ext_ref,
                       *, tb, inner):
    xb = x_ref[...].astype(BF16)

    @pl.when(pl.program_id(1) == 0)
    def _():
        ext_ref[0:HALO, :] = jnp.zeros((HALO, inner), F32)

    xm = _dot(xb, wi_ref[:, :inner])
    ext_ref[HALO:HALO + tb, :] = xm
    z = _dot(xb, wi_ref[:, inner:])
    sz = z * _sigmoid(z)
    gz_ref[...] = (ng_ref[...] * sz).astype(BF16)
    cw = cw_ref[...]
    conv = cw[ML_CONV - 1:ML_CONV, :] * xm + cb_ref[...]
    for j in range(1, ML_CONV):
        conv = conv + cw[ML_CONV - 1 - j:ML_CONV - j, :] * ext_ref[HALO - j:HALO - j + tb, :]
    ext_ref[0:HALO, :] = ext_ref[tb:tb + HALO, :]
    xc = conv * _sigmoid(conv)
    xcb = xc.astype(BF16)
    xmb = xm.astype(BF16)
    cz_ref[...] = (skip_ref[...] * xc * sz).astype(BF16)

    gates = jnp.broadcast_to(bg_ref[...], (tb, LANES))
    for g in range(inner // MXU_TILE):
        sl = slice(g * MXU_TILE, (g + 1) * MXU_TILE)
        qk = _dot(xcb[:, sl], wqk_ref[g])
        qg = qk[:, :MXU_TILE].astype(BF16)
        kg = qk[:, MXU_TILE:].astype(BF16)
        vg = _dot(xmb[:, sl], wv_ref[g]).astype(BF16)
        q_ref[:, sl] = qg
        kt_ref[sl, :] = _dot_nt(wkt_ref[g], xcb[:, sl]).astype(BF16)
        v_ref[:, sl] = vg
        gates = (gates + _dot(qg, wg_ref[sl, :])
                 + _dot(kg, wg_ref[inner + g * MXU_TILE:inner + (g + 1) * MXU_TILE, :])
                 + _dot(vg, wg_ref[2 * inner + g * MXU_TILE:2 * inner + (g + 1) * MXU_TILE, :]))
    gt_ref[...] = gates


def _block_diag_tiles(w, tile):
    nb, c, d = w.shape
    assert c == d and tile % c == 0 and nb % (tile // c) == 0
    cols = lax.broadcasted_iota(jnp.int32, (d, tile), 1)
    expand = (cols % d == lax.broadcasted_iota(jnp.int32, (d, tile), 0)).astype(F32)
    tiled = jnp.dot(w.reshape(nb * c, d), expand, precision=lax.Precision.HIGHEST)
    r = lax.broadcasted_iota(jnp.int32, (tile, tile), 0) // c
    s = lax.broadcasted_iota(jnp.int32, (tile, tile), 1) // d
    return jnp.where(r == s, tiled.reshape(nb * c // tile, tile, tile), 0.0).astype(BF16)


def _mlstm_proj(x2d, w_in, conv_w, conv_b, w_q, w_k, w_v, w_gates, b_gates, skip, norm_g,
                *, batch, tb):
    t, d = x2d.shape
    inner = w_in.shape[1] // 2
    seq = t // batch
    assert seq % tb == 0 and inner % MXU_TILE == 0 and tb >= HALO
    nblk = seq // tb
    ntile = inner // MXU_TILE
    wqk = jnp.concatenate([_block_diag_tiles(w_q, MXU_TILE), _block_diag_tiles(w_k, MXU_TILE)],
                          axis=-1)
    wkt = _block_diag_tiles(jnp.swapaxes(w_k, 1, 2), MXU_TILE)
    wv = _block_diag_tiles(w_v, MXU_TILE)
    ng = w_gates.shape[1]
    wg = jnp.pad(w_gates, ((0, 0), (0, LANES - ng))).astype(BF16)
    bg = jnp.pad(b_gates, (0, LANES - ng)).reshape(1, LANES)
    tok = lambda w: pl.BlockSpec((tb, w), lambda b, j: (b * nblk + j, 0))
    tok_t = pl.BlockSpec((None, inner, tb), lambda b, j: (b, 0, j))
    act = jax.ShapeDtypeStruct((t, inner), BF16)
    act_t = jax.ShapeDtypeStruct((batch, inner, seq), BF16)
    return pl.pallas_call(
        functools.partial(_mlstm_proj_kernel, tb=tb, inner=inner),
        out_shape=(act, act_t, act, act, act, jax.ShapeDtypeStruct((t, LANES), F32)),
        grid=(batch, nblk),
        in_specs=[tok(d), _resident((d, 2 * inner)),
                  _resident((ML_CONV, inner)), _resident((1, inner)),
                  _resident((ntile, MXU_TILE, 2 * MXU_TILE)), _resident((ntile, MXU_TILE, MXU_TILE)),
                  _resident((ntile, MXU_TILE, MXU_TILE)),
                  _resident((3 * inner, LANES)), _resident((1, LANES)),
                  _resident((1, inner)), _resident((1, inner))],
        out_specs=(tok(inner), tok_t, tok(inner), tok(inner), tok(inner), tok(LANES)),
        scratch_shapes=[pltpu.VMEM((tb + HALO, inner), F32)],
        compiler_params=_params(("arbitrary", "arbitrary")),
        name="mlstm_proj",
    )(x2d, w_in, conv_w, conv_b.reshape(1, inner), wqk, wkt, wv, wg, bg,
      skip.reshape(1, inner), norm_g.reshape(1, inner))


def _split3(a):
    hi = a.astype(BF16)
    r1 = a - hi.astype(F32)
    mid = r1.astype(BF16)
    lo = (r1 - mid.astype(F32)).astype(BF16)
    return hi, mid, lo


CELL_CHUNK = 256
ROW_BLOCK = 64


def _mlstm_cell_kernel(q_ref, kt_ref, v_ref, cz_ref, gz_ref, gt_ref, x_ref,
                       wo_ref, g1_ref, b1_ref, o_ref,
                       c_ref, n_ref, m_ref, hg_ref, cb_ref, cu_ref, nu_ref, dec_ref, *, nb, tb, dh, lc):
    nh = ML_HEADS
    scale = dh ** -0.5
    log_scale = -0.5 * math.log(dh)

    @pl.when(pl.program_id(1) == 0)
    def _():
        for ref in (c_ref, n_ref, m_ref, cu_ref, nu_ref, dec_ref):
            ref[...] = jnp.zeros(ref.shape, F32)

    row = lax.broadcasted_iota(jnp.int32, (lc, lc), 0)
    col = lax.broadcasted_iota(jnp.int32, (lc, lc), 1)
    causal = col <= row
    tri = jnp.where(causal, 1.0, 0.0).astype(BF16)
    lane = lax.broadcasted_iota(jnp.int32, (lc, LANES), 1)

    for c in range(tb // lc):
        rows = slice(c * lc, (c + 1) * lc)
        for b in range(nb):
            gt = gt_ref[b, rows, :]
            lf = jnp.minimum(gt, 0.0) - jnp.log(1.0 + jnp.exp(-jnp.abs(gt)))
            hi, mid, lo = _split3(lf)
            bcum = _dot(tri, hi) + _dot(tri, mid) + _dot(tri, lo)
            colmat = jnp.where(lane < nh, gt, bcum)
            rowmat = colmat.T

            def early(h):
                hs = slice(h * dh, (h + 1) * dh)
                hd = dict(st=b * nh + h, hs=hs,
                          i_row=rowmat[h:h + 1, :], b_row=rowmat[nh + h:nh + h + 1, :],
                          b_col=colmat[:, nh + h:nh + h + 1],
                          q=q_ref[b, rows, hs], kt=kt_ref[b, hs, rows], v=v_ref[b, rows, hs])
                st = hd["st"]
                hd["m_prev"] = m_ref[st:st + 1, 0:1]
                dec = dec_ref[st:st + 1, 0:1]
                for r0 in range(0, dh, ROW_BLOCK):
                    rb = slice(r0, r0 + ROW_BLOCK)
                    c_in = dec * c_ref[st, rb, :] + cu_ref[st, rb, :]
                    c_ref[st, rb, :] = c_in
                    cb_ref[st, rb, :] = c_in.astype(BF16)
                hd["n_prev"] = dec * n_ref[st] + nu_ref[st]
                n_ref[st] = hd["n_prev"]
                hd["qk"] = _dot(hd["q"], hd["kt"])
                return hd

            heads = [early(h) for h in range(nh)]

            def update(hd):
                b_last = hd["b_row"][:, lc - 1:lc]
                g_row = b_last - hd["b_row"] + hd["i_row"]
                m_new = jnp.maximum(b_last + hd["m_prev"], jnp.max(g_row, axis=-1, keepdims=True))
                wg_row = jnp.exp(g_row - m_new)
                hd["decay"] = jnp.exp(b_last + hd["m_prev"] - m_new)
                hd["m_new"] = m_new
                kw = (hd["kt"].astype(F32) * wg_row).astype(BF16)
                hd["c_upd"] = _dot(kw, hd["v"])
                hd["n_upd"] = _dot_nt(jnp.ones((8, lc), BF16), kw)

            def output(hd):
                st, hs = hd["st"], hd["hs"]
                dmat = jnp.where(causal, hd["b_col"] - hd["b_row"] + hd["i_row"], -jnp.inf)
                m_inter = hd["b_col"] + hd["m_prev"]
                m_t = jnp.maximum(m_inter, jnp.max(dmat, axis=-1, keepdims=True))
                s = hd["qk"] * jnp.exp(dmat - (m_t - log_scale))
                si = jnp.exp(m_inter - m_t) * scale
                qs = hd["q"].astype(F32) * si
                num = _dot(s.astype(BF16), hd["v"]) + _dot(qs.astype(BF16), cb_ref[st])
                den = (jnp.sum(s, axis=-1, keepdims=True)
                       + jnp.sum(qs * hd["n_prev"], axis=-1, keepdims=True))
                r = 1.0 / jnp.maximum(jnp.abs(den), jnp.exp(-m_t))
                nc = num - jnp.mean(num, axis=-1, keepdims=True)
                var = jnp.mean(nc * nc, axis=-1, keepdims=True)
                hn = nc * (r * lax.rsqrt(r * r * var + LN_EPS))
                hg = hn * gz_ref[b, rows, hs].astype(F32) + cz_ref[b, rows, hs].astype(F32)
                hg_ref[b, rows, hs] = hg.astype(BF16)

            def advance(hd):
                st = hd["st"]
                cu_ref[st] = hd["c_upd"]
                nu_ref[st] = hd["n_upd"][0:1, :]
                dec_ref[st:st + 1, :] = jnp.broadcast_to(hd["decay"], (1, LANES))
                m_ref[st:st + 1, :] = jnp.broadcast_to(hd["m_new"], (1, LANES))

            for hd in heads:
                update(hd)
            for hd in heads:
                advance(hd)
            for hd in heads:
                output(hd)

            y = _dot(hg_ref[b, rows, :], wo_ref[...])
            o_ref[b, rows, :] = _deepnorm(x_ref[b, rows, :], y, g1_ref[...], b1_ref[...])


def _mlstm_cell_layer(q, kt, v, cz, gz, gt, x3d, w_out, g1, b1, *, layer, nb, tb, casts=()):
    batch, seq, inner = q.shape
    d = x3d.shape[-1]
    dh = inner // ML_HEADS
    assert seq % tb == 0 and tb % CELL_CHUNK == 0 and batch % nb == 0
    tok = lambda w: pl.BlockSpec((nb, tb, w), lambda bi, j: (bi, j, 0))
    tok_t = pl.BlockSpec((nb, inner, tb), lambda bi, j: (bi, 0, j))
    nstate = nb * ML_HEADS
    nblk = seq // tb
    c_in, c_out, c_shapes = _side_cast_specs(casts, (batch // nb) * nblk, lambda bi, j: bi * nblk + j)
    in_specs = [tok(inner), tok_t, tok(inner), tok(inner), tok(inner), tok(LANES), tok(d),
                _resident((inner, d)), _resident((1, d), layer), _resident((1, d), layer)]
    return pl.pallas_call(
        _with_side_casts(functools.partial(_mlstm_cell_kernel, nb=nb, tb=tb, dh=dh, lc=CELL_CHUNK),
                         len(in_specs), 1, casts),
        out_shape=(jax.ShapeDtypeStruct((batch, seq, d), F32), *c_shapes),
        grid=(batch // nb, nblk),
        in_specs=in_specs + c_in,
        out_specs=(tok(d), *c_out),
        scratch_shapes=[pltpu.VMEM((nstate, dh, dh), F32),
                        pltpu.VMEM((nstate, 1, dh), F32),
                        pltpu.VMEM((-(-nstate // 8) * 8, LANES), F32),
                        pltpu.VMEM((nb, tb, inner), BF16),
                        pltpu.VMEM((nstate, dh, dh), BF16),
                        pltpu.VMEM((nstate, dh, dh), F32),
                        pltpu.VMEM((nstate, 1, dh), F32),
                        pltpu.VMEM((-(-nstate // 8) * 8, LANES), F32)],
        compiler_params=_params(("arbitrary", "arbitrary")),
        name="mlstm_cell",
    )(q, kt, v, cz, gz, gt, x3d, w_out, g1, b1, *[c.w for c in casts])


def kernel(x, p, gm_w_in, gm_ln_g, gm_ln_b, gm_w_s, gm_b_s, gm_w_out, ml_w_in, ml_conv_w, ml_conv_b, ml_w_q, ml_w_k, ml_w_v, ml_w_gates, ml_b_gates, ml_skip, ml_norm_g, ml_w_out, ln1_g, ln1_b, ln2_g, ln2_b, ffn_w_gate, ffn_w_up, ffn_w_down, ple_w_proj, ple_w_gate, ple_b_gate):
    batch, seq, d = x.shape
    t = batch * seq
    xs = x.reshape(t, d)
    ps = p.reshape(p.shape[0], t, p.shape[-1])
    vec = lambda a: a.reshape(a.shape[0], 1, a.shape[-1])
    ln1g, ln1b, ln2g, ln2b, pbg = vec(ln1_g), vec(ln1_b), vec(ln2_g), vec(ln2_b), vec(ple_b_gate)

    def ffn_casts(i):
        return (_SideCast(ffn_w_gate, i), _SideCast(ffn_w_up, i),
                _SideCast(ffn_w_down, i, 1.0 / ALPHA), _SideCast(ple_w_gate, i),
                _SideCast(ple_w_proj, i))

    def ffn(xs, i, weights, casts=()):
        return _ffn_layer(xs, ps, *weights[:3], ln2g, ln2b, weights[3], pbg, weights[4],
                          layer=i, tb=512, casts=casts)

    xs, *ffn0_w = _gmlp_layer(xs, gm_w_in[0], gm_ln_g[0], gm_ln_b[0], gm_w_s[0], gm_b_s[0],
                              gm_w_out[0], ln1g, ln1b, layer=0, tb=512, casts=ffn_casts(0))
    xs, ml_w_in_b, ml_w_out_b = ffn(xs, 0, ffn0_w, casts=(_SideCast(ml_w_in, 0),
                                                          _SideCast(ml_w_out, 0, 1.0 / ALPHA)))
    q, kt, v, cz, gz, gt = _mlstm_proj(xs, ml_w_in_b, ml_conv_w[0], ml_conv_b[0], ml_w_q[0],
                                       ml_w_k[0], ml_w_v[0], ml_w_gates[0], ml_b_gates[0],
                                       ml_skip[0], ml_norm_g[0], batch=batch, tb=512)
    seq3 = lambda a: a.reshape(batch, seq, a.shape[-1])
    xs, *ffn1_w = _mlstm_cell_layer(seq3(q), kt, seq3(v), seq3(cz), seq3(gz), seq3(gt), seq3(xs),
                                    ml_w_out_b, ln1g, ln1b, layer=1, nb=1, tb=256,
                                    casts=ffn_casts(1))
    xs, = ffn(xs.reshape(t, d), 1, ffn1_w)
    return xs.reshape(batch, seq, d)
```

```python
import functools
import math
from typing import NamedTuple

import jax
import jax.numpy as jnp
from jax import lax
from jax.experimental import pallas as pl
from jax.experimental.pallas import tpu as pltpu

F32 = jnp.float32
BF16 = jnp.bfloat16

CHUNK = 128
LN_EPS = 1e-5
DEPTH = 2
ALPHA = (2 * DEPTH) ** 0.25
GM_GROUPS = 8
ML_HEADS = 4
ML_CONV = 4
ML_QKV_BLOCK = 4

LANES = 128
MXU_TILE = 256
VMEM_LIMIT_BYTES = 56 * 1024 * 1024


def _dot(a, b):
    return jnp.dot(a, b, preferred_element_type=F32)


def _sigmoid(x):
    return 1.0 / (1.0 + jnp.exp(-x))


def _gelu_tanh(x):
    c = 0.7978845608028654
    hx = 0.5 * x
    return hx + hx * jnp.tanh(x * (c + (c * 0.044715) * (x * x)))


def _layer_norm(x, g, b, eps=LN_EPS):
    mu = jnp.mean(x, axis=-1, keepdims=True)
    xc = x - mu
    var = jnp.mean(xc * xc, axis=-1, keepdims=True)
    return xc * lax.rsqrt(var + eps) * g + b


def _deepnorm(x, y_over_alpha, g, b):
    return _layer_norm(x + y_over_alpha, g, b, eps=LN_EPS / ALPHA ** 2)


def _layer_norm_one_pass(x, g, b):
    inv_n = 1.0 / x.shape[-1]
    mu = jnp.sum(x, axis=-1, keepdims=True) * inv_n
    var = jnp.sum(x * x, axis=-1, keepdims=True) * inv_n - mu * mu
    return (x - mu) * lax.rsqrt(var + LN_EPS) * g + b


def _resident(shape, layer=None):
    nd = len(shape)
    if layer is None:
        return pl.BlockSpec(shape, lambda *_: (0,) * nd, pipeline_mode=pl.Buffered(1))
    return pl.BlockSpec((None,) + tuple(shape), lambda *_: (layer,) + (0,) * nd,
                        pipeline_mode=pl.Buffered(1))


class _SideCast(NamedTuple):
    w: jax.Array
    layer: int | None = None
    scale: float = 1.0


BF16_SUBLANES = 16


def _side_cast_specs(casts, steps, step_of):
    in_specs, out_specs, out_shapes = [], [], []
    for c in casts:
        k, n = c.w.shape[-2:]
        cuts = [(k // r, n // cc) for r in range(BF16_SUBLANES, k + 1, BF16_SUBLANES) if k % r == 0
                for cc in range(LANES, n + 1, LANES) if n % cc == 0
                if steps % ((k // r) * (n // cc)) == 0]
        nr, ncol = max(cuts, key=lambda rc: (rc[0] * rc[1], -rc[1]))
        per = steps // (nr * ncol)

        def cell(*g, per=per, ncol=ncol):
            i = step_of(*g) // per
            return i // ncol, i % ncol

        block = (k // nr, n // ncol)
        if c.layer is None:
            in_specs.append(pl.BlockSpec(block, cell))
        else:
            in_specs.append(pl.BlockSpec((None,) + block, lambda *g, cell=cell, l=c.layer: (l, *cell(*g))))
        out_specs.append(pl.BlockSpec(block, cell))
        out_shapes.append(jax.ShapeDtypeStruct((k, n), BF16))
    return in_specs, out_specs, out_shapes


def _with_side_casts(body, n_in, n_out, casts):
    nc = len(casts)

    def kernel(*refs):
        cast_in = refs[n_in:n_in + nc]
        cast_out = refs[n_in + nc + n_out:n_in + 2 * nc + n_out]
        for c, src, dst in zip(casts, cast_in, cast_out):
            w = src[...]
            dst[...] = (w if c.scale == 1.0 else w * c.scale).astype(BF16)
        body(*refs[:n_in], *refs[n_in + nc:n_in + nc + n_out], *refs[n_in + 2 * nc + n_out:])

    return kernel


def _params(semantics):
    return pltpu.CompilerParams(dimension_semantics=semantics,
                                vmem_limit_bytes=VMEM_LIMIT_BYTES)


def _gmlp_kernel(x_ref, wi_ref, lng_ref, lnb_ref, ws_ref, bs_ref, wo_ref,
                 g1_ref, b1_ref, o_ref, vn_ref, y_ref, *, tb, half, layer):
    gdim = half // GM_GROUPS
    x = x_ref[...]
    xb = x.astype(BF16)
    pair = 2 * gdim
    u_pair = lambda j: _gelu_tanh(_dot(xb, wi_ref[:, j * pair:(j + 1) * pair]))
    v = _gelu_tanh(_dot(xb, wi_ref[:, half:]))
    u_next = u_pair(0)
    vn_ref[...] = _layer_norm_one_pass(v, lng_ref[...], lnb_ref[...]).astype(BF16)

    row = lax.broadcasted_iota(jnp.int32, (CHUNK, CHUNK), 0)
    col = lax.broadcasted_iota(jnp.int32, (CHUNK, CHUNK), 1)
    causal = col <= row
    bs = bs_ref[...]
    for j in range(GM_GROUPS // 2):
        u2 = u_next if j == 0 else u_pair(j)
        for gg in range(2):
            g = 2 * j + gg
            wsg = jnp.where(causal, ws_ref[g], 0.0).astype(BF16)
            bg = bs[:, g:g + 1]
            for c in range(tb // CHUNK):
                rows = slice(c * CHUNK, (c + 1) * CHUNK)
                sv = _dot(wsg, vn_ref[rows, g * gdim:(g + 1) * gdim]) + bg
                y_ref[rows, g * gdim:(g + 1) * gdim] = (
                    u2[rows, gg * gdim:(gg + 1) * gdim] * sv).astype(BF16)
    for r in range(2):
        rows = slice(r * (tb // 2), (r + 1) * (tb // 2))
        y = _dot(y_ref[rows, :], wo_ref[...])
        o_ref[rows, :] = _deepnorm(x_ref[rows, :], y, g1_ref[layer:layer + 1, :],
                                   b1_ref[layer:layer + 1, :])


def _gmlp_layer(x2d, w_in, ln_g, ln_b, w_s, b_s, w_out, g1, b1, *, layer, tb, casts=()):
    t, d = x2d.shape
    half = w_in.shape[1] // 2
    assert t % tb == 0 and tb % CHUNK == 0 and half % (GM_GROUPS * LANES) == 0
    row = lambda a: a.reshape(1, -1)
    tok = pl.BlockSpec((tb, d), lambda i: (i, 0))
    c_in, c_out, c_shapes = _side_cast_specs(casts, t // tb, lambda i: i)
    in_specs = [tok, _resident((d, 2 * half)),
                _resident((1, half)), _resident((1, half)),
                _resident(w_s.shape), _resident((CHUNK, GM_GROUPS)),
                _resident((half, d)), _resident(g1.shape), _resident(b1.shape)]
    return pl.pallas_call(
        _with_side_casts(functools.partial(_gmlp_kernel, tb=tb, half=half, layer=layer), len(in_specs), 1, casts),
        out_shape=(jax.ShapeDtypeStruct((t, d), F32), *c_shapes),
        grid=(t // tb,),
        in_specs=in_specs + c_in,
        out_specs=(tok, *c_out),
        scratch_shapes=[pltpu.VMEM((tb, half), BF16), pltpu.VMEM((tb, half), BF16)],
        compiler_params=_params(("arbitrary",)),
        name="gmlp_layer",
    )(x2d, w_in.astype(BF16), row(ln_g), row(ln_b), w_s, b_s.T,
      (w_out * (1.0 / ALPHA)).astype(BF16), g1, b1, *[c.w for c in casts])


def _ffn_kernel(x_ref, p_ref, wg_ref, wu_ref, wd_ref, g2_ref, b2_ref,
                wpg_ref, bpg_ref, wpp_ref, o_ref, *, layer):
    x = x_ref[...]
    xb = x.astype(BF16)
    dff = wg_ref.shape[1]
    cut = (dff // MXU_TILE // 2) * MXU_TILE
    y = None
    for cols in (slice(0, cut), slice(cut, dff)):
        g = _dot(xb, wg_ref[:, cols])
        u = _dot(xb, wu_ref[:, cols])
        h = (g * _sigmoid(g) * u).astype(BF16)
        part = _dot(h, wd_ref[cols, :])
        y = part if y is None else y + part
    pp = _dot(p_ref[...].astype(BF16), wpp_ref[...])
    vec = slice(layer, layer + 1)
    x2 = _deepnorm(x, y, g2_ref[vec, :], b2_ref[vec, :])
    gate = _sigmoid(_dot(x2.astype(BF16), wpg_ref[...]) + bpg_ref[vec, :])
    o_ref[...] = x2 + gate * pp


def _ffn_layer(x2d, p3d, w_gate, w_up, w_down, g2, b2, wpg, bpg, wpp, *, layer, tb, casts=()):
    t, d = x2d.shape
    dff = w_gate.shape[-1]
    dp = p3d.shape[-1]
    assert t % tb == 0
    tok = pl.BlockSpec((tb, d), lambda i: (i, 0))
    c_in, c_out, c_shapes = _side_cast_specs(casts, t // tb, lambda i: i)
    in_specs = [tok, pl.BlockSpec((None, tb, dp), lambda i: (layer, i, 0)),
                _resident((d, dff)), _resident((d, dff)), _resident((dff, d)),
                _resident(g2.shape), _resident(b2.shape),
                _resident((d, d)), _resident(bpg.shape), _resident((dp, d))]
    return pl.pallas_call(
        _with_side_casts(functools.partial(_ffn_kernel, layer=layer), len(in_specs), 1, casts),
        out_shape=(jax.ShapeDtypeStruct((t, d), F32), *c_shapes),
        grid=(t // tb,),
        in_specs=in_specs + c_in,
        out_specs=(tok, *c_out),
        compiler_params=_params(("arbitrary",)),
        name="ffn_layer",
    )(x2d, p3d, w_gate, w_up, w_down, g2, b2, wpg, bpg, wpp, *[c.w for c in casts])


HALO = 8


def _dot_nt(a, b):
    return lax.dot_general(a, b, (((1,), (1,)), ((), ())), preferred_element_type=F32)


def _mlstm_proj_kernel(x_ref, wi_ref, cw_ref, cb_ref, wqk_ref, wkt_ref, wv_ref, wg_ref, bg_ref,
                       skip_ref, ng_ref, q_ref, kt_ref, v_ref, cz_ref, gz_ref, gt_ref, ext_ref,
                       *, tb, inner):
    xb = x_ref[...].astype(BF16)

    @pl.when(pl.program_id(1) == 0)
    def _():
        ext_ref[0:HALO, :] = jnp.zeros((HALO, inner), F32)

    xm = _dot(xb, wi_ref[:, :inner])
    ext_ref[HALO:HALO + tb, :] = xm
    z = _dot(xb, wi_ref[:, inner:])
    sz = z * _sigmoid(z)
    gz_ref[...] = (ng_ref[...] * sz).astype(BF16)
    cw = cw_ref[...]
    conv = cw[ML_CONV - 1:ML_CONV, :] * xm + cb_ref[...]
    for j in range(1, ML_CONV):
        conv = conv + cw[ML_CONV - 1 - j:ML_CONV - j, :] * ext_ref[HALO - j:HALO - j + tb, :]
    ext_ref[0:HALO, :] = ext_ref[tb:tb + HALO, :]
    xc = conv * _sigmoid(conv)
    xcb = xc.astype(BF16)
    xmb = xm.astype(BF16)
    cz_ref[...] = (skip_ref[...] * xc * sz).astype(BF16)

    gates = jnp.broadcast_to(bg_ref[...], (tb, LANES))
    for g in range(inner // MXU_TILE):
        sl = slice(g * MXU_TILE, (g + 1) * MXU_TILE)
        qk = _dot(xcb[:, sl], wqk_ref[g])
        qg = qk[:, :MXU_TILE].astype(BF16)
        kg = qk[:, MXU_TILE:].astype(BF16)
        vg = _dot(xmb[:, sl], wv_ref[g]).astype(BF16)
        q_ref[:, sl] = qg
        kt_ref[sl, :] = _dot_nt(wkt_ref[g], xcb[:, sl]).astype(BF16)
        v_ref[:, sl] = vg
        gates = (gates + _dot(qg, wg_ref[sl, :])
                 + _dot(kg, wg_ref[inner + g * MXU_TILE:inner + (g + 1) * MXU_TILE, :])
                 + _dot(vg, wg_ref[2 * inner + g * MXU_TILE:2 * inner + (g + 1) * MXU_TILE, :]))
    gt_ref[...] = gates


def _block_diag_tiles(w, tile):
    nb, c, d = w.shape
    assert c == d and tile % c == 0 and nb % (tile // c) == 0
    cols = lax.broadcasted_iota(jnp.int32, (d, tile), 1)
    expand = (cols % d == lax.broadcasted_iota(jnp.int32, (d, tile), 0)).astype(F32)
    tiled = jnp.dot(w.reshape(nb * c, d), expand, precision=lax.Precision.HIGHEST)
    r = lax.broadcasted_iota(jnp.int32, (tile, tile), 0) // c
    s = lax.broadcasted_iota(jnp.int32, (tile, tile), 1) // d
    return jnp.where(r == s, tiled.reshape(nb * c // tile, tile, tile), 0.0).astype(BF16)


def _mlstm_proj(x2d, w_in, conv_w, conv_b, w_q, w_k, w_v, w_gates, b_gates, skip, norm_g,
                *, batch, tb):
    t, d = x2d.shape
    inner = w_in.shape[1] // 2
    seq = t // batch
    assert seq % tb == 0 and inner % MXU_TILE == 0 and tb >= HALO
    nblk = seq // tb
    ntile = inner // MXU_TILE
    wqk = jnp.concatenate([_block_diag_tiles(w_q, MXU_TILE), _block_diag_tiles(w_k, MXU_TILE)],
                          axis=-1)
    wkt = _block_diag_tiles(jnp.swapaxes(w_k, 1, 2), MXU_TILE)
    wv = _block_diag_tiles(w_v, MXU_TILE)
    ng = w_gates.shape[1]
    wg = jnp.pad(w_gates, ((0, 0), (0, LANES - ng))).astype(BF16)
    bg = jnp.pad(b_gates, (0, LANES - ng)).reshape(1, LANES)
    tok = lambda w: pl.BlockSpec((tb, w), lambda b, j: (b * nblk + j, 0))
    tok_t = pl.BlockSpec((None, inner, tb), lambda b, j: (b, 0, j))
    act = jax.ShapeDtypeStruct((t, inner), BF16)
    act_t = jax.ShapeDtypeStruct((batch, inner, seq), BF16)
    return pl.pallas_call(
        functools.partial(_mlstm_proj_kernel, tb=tb, inner=inner),
        out_shape=(act, act_t, act, act, act, jax.ShapeDtypeStruct((t, LANES), F32)),
        grid=(batch, nblk),
        in_specs=[tok(d), _resident((d, 2 * inner)),
                  _resident((ML_CONV, inner)), _resident((1, inner)),
                  _resident((ntile, MXU_TILE, 2 * MXU_TILE)), _resident((ntile, MXU_TILE, MXU_TILE)),
                  _resident((ntile, MXU_TILE, MXU_TILE)),
                  _resident((3 * inner, LANES)), _resident((1, LANES)),
                  _resident((1, inner)), _resident((1, inner))],
        out_specs=(tok(inner), tok_t, tok(inner), tok(inner), tok(inner), tok(LANES)),
        scratch_shapes=[pltpu.VMEM((tb + HALO, inner), F32)],
        compiler_params=_params(("arbitrary", "arbitrary")),
        name="mlstm_proj",
    )(x2d, w_in, conv_w, conv_b.reshape(1, inner), wqk, wkt, wv, wg, bg,
      skip.reshape(1, inner), norm_g.reshape(1, inner))


def _split3(a):
    hi = a.astype(BF16)
    r1 = a - hi.astype(F32)
    mid = r1.astype(BF16)
    lo = (r1 - mid.astype(F32)).astype(BF16)
    return hi, mid, lo


CELL_CHUNK = 256

def _mlstm_cell_kernel(q_ref, kt_ref, v_ref, cz_ref, gz_ref, gt_ref, x_ref,
                       wo_ref, g1_ref, b1_ref, o_ref,
                       c_ref, n_ref, m_ref, hg_ref, cb_ref, *, nb, tb, dh, lc, layer):
    nh = ML_HEADS
    scale = dh ** -0.5
    log_scale = -0.5 * math.log(dh)

    @pl.when(pl.program_id(1) == 0)
    def _():
        for ref in (c_ref, n_ref, m_ref):
            ref[...] = jnp.zeros(ref.shape, F32)

    row = lax.broadcasted_iota(jnp.int32, (lc, lc), 0)
    col = lax.broadcasted_iota(jnp.int32, (lc, lc), 1)
    causal = col <= row
    tri = jnp.where(causal, 1.0, 0.0).astype(BF16)
    lane = lax.broadcasted_iota(jnp.int32, (lc, LANES), 1)

    for c in range(tb // lc):
        rows = slice(c * lc, (c + 1) * lc)
        for b in range(nb):
            gt = gt_ref[b, rows, :]
            lf = jnp.minimum(gt, 0.0) - jnp.log(1.0 + jnp.exp(-jnp.abs(gt)))
            hi, mid, lo = _split3(lf)
            bcum = _dot(tri, hi) + _dot(tri, mid) + _dot(tri, lo)
            colmat = jnp.where(lane < nh, gt, bcum)
            rowmat = colmat.T

            def early(h):
                hs = slice(h * dh, (h + 1) * dh)
                hd = dict(st=b * nh + h, hs=hs,
                          i_row=rowmat[h:h + 1, :], b_row=rowmat[nh + h:nh + h + 1, :],
                          b_col=colmat[:, nh + h:nh + h + 1],
                          q=q_ref[b, rows, hs], kt=kt_ref[b, hs, rows], v=v_ref[b, rows, hs])
                st = hd["st"]
                hd["m_prev"] = m_ref[st:st + 1, 0:1]
                hd["n_prev"] = n_ref[st]
                cb_ref[st] = c_ref[st].astype(BF16)
                hd["qk"] = _dot(hd["q"], hd["kt"])
                return hd

            heads = [early(h) for h in range(nh)]

            def update(hd):
                b_last = hd["b_row"][:, lc - 1:lc]
                g_row = b_last - hd["b_row"] + hd["i_row"]
                m_new = jnp.maximum(b_last + hd["m_prev"], jnp.max(g_row, axis=-1, keepdims=True))
                wg_row = jnp.exp(g_row - m_new)
                hd["decay"] = jnp.exp(b_last + hd["m_prev"] - m_new)
                hd["m_new"] = m_new
                kw = (hd["kt"].astype(F32) * wg_row).astype(BF16)
                hd["c_upd"] = _dot(kw, hd["v"])
                hd["n_upd"] = _dot_nt(jnp.ones((8, lc), BF16), kw)

            def output(hd):
                st, hs = hd["st"], hd["hs"]
                dmat = jnp.where(causal, hd["b_col"] - hd["b_row"] + hd["i_row"], -jnp.inf)
                m_inter = hd["b_col"] + hd["m_prev"]
                m_t = jnp.maximum(m_inter, jnp.max(dmat, axis=-1, keepdims=True))
                s = hd["qk"] * jnp.exp(dmat - (m_t - log_scale))
                si = jnp.exp(m_inter - m_t) * scale
                qs = hd["q"].astype(F32) * si
                num = _dot(s.astype(BF16), hd["v"]) + _dot(qs.astype(BF16), cb_ref[st])
                den = (jnp.sum(s, axis=-1, keepdims=True)
                       + jnp.sum(qs * hd["n_prev"], axis=-1, keepdims=True))
                r = 1.0 / jnp.maximum(jnp.abs(den), jnp.exp(-m_t))
                nc = num - jnp.mean(num, axis=-1, keepdims=True)
                var = jnp.mean(nc * nc, axis=-1, keepdims=True)
                hn = nc * (r * lax.rsqrt(r * r * var + LN_EPS))
                hg = hn * gz_ref[b, rows, hs].astype(F32) + cz_ref[b, rows, hs].astype(F32)
                hg_ref[b, rows, hs] = hg.astype(BF16)

            def advance(hd):
                st = hd["st"]
                c_ref[st] = hd["decay"] * c_ref[st] + hd["c_upd"]
                n_ref[st] = hd["decay"] * hd["n_prev"] + hd["n_upd"][0:1, :]
                m_ref[st:st + 1, :] = jnp.broadcast_to(hd["m_new"], (1, LANES))

            for hd in heads:
                update(hd)
            for hd in heads:
                advance(hd)
            for hd in heads:
                output(hd)

            y = _dot(hg_ref[b, rows, :], wo_ref[...])
            o_ref[b, rows, :] = _deepnorm(x_ref[b, rows, :], y, g1_ref[layer:layer + 1, :],
                                          b1_ref[layer:layer + 1, :])


def _mlstm_cell_layer(q, kt, v, cz, gz, gt, x3d, w_out, g1, b1, *, layer, nb, tb, casts=()):
    batch, seq, inner = q.shape
    d = x3d.shape[-1]
    dh = inner // ML_HEADS
    assert seq % tb == 0 and tb % CELL_CHUNK == 0 and batch % nb == 0
    tok = lambda w: pl.BlockSpec((nb, tb, w), lambda bi, j: (bi, j, 0))
    tok_t = pl.BlockSpec((nb, inner, tb), lambda bi, j: (bi, 0, j))
    nstate = nb * ML_HEADS
    nblk = seq // tb
    c_in, c_out, c_shapes = _side_cast_specs(casts, (batch // nb) * nblk, lambda bi, j: bi * nblk + j)
    in_specs = [tok(inner), tok_t, tok(inner), tok(inner), tok(inner), tok(LANES), tok(d),
                _resident((inner, d)), _resident(g1.shape), _resident(b1.shape)]
    return pl.pallas_call(
        _with_side_casts(functools.partial(_mlstm_cell_kernel, nb=nb, tb=tb, dh=dh, lc=CELL_CHUNK, layer=layer),
                         len(in_specs), 1, casts),
        out_shape=(jax.ShapeDtypeStruct((batch, seq, d), F32), *c_shapes),
        grid=(batch // nb, nblk),
        in_specs=in_specs + c_in,
        out_specs=(tok(d), *c_out),
        scratch_shapes=[pltpu.VMEM((nstate, dh, dh), F32),
                        pltpu.VMEM((nstate, 1, dh), F32),
                        pltpu.VMEM((-(-nstate // 8) * 8, LANES), F32),
                        pltpu.VMEM((nb, tb, inner), BF16),
                        pltpu.VMEM((nstate, dh, dh), BF16)],
        compiler_params=_params(("arbitrary", "arbitrary")),
        name="mlstm_cell",
    )(q, kt, v, cz, gz, gt, x3d, w_out, g1, b1, *[c.w for c in casts])


def kernel(x, p, gm_w_in, gm_ln_g, gm_ln_b, gm_w_s, gm_b_s, gm_w_out, ml_w_in, ml_conv_w, ml_conv_b, ml_w_q, ml_w_k, ml_w_v, ml_w_gates, ml_b_gates, ml_skip, ml_norm_g, ml_w_out, ln1_g, ln1_b, ln2_g, ln2_b, ffn_w_gate, ffn_w_up, ffn_w_down, ple_w_proj, ple_w_gate, ple_b_gate):
    batch, seq, d = x.shape
    t = batch * seq
    xs = x.reshape(t, d)
    ps = p.reshape(p.shape[0], t, p.shape[-1])
    ln1g, ln1b, ln2g, ln2b, pbg = ln1_g, ln1_b, ln2_g, ln2_b, ple_b_gate

    def ffn_casts(i):
        return (_SideCast(ffn_w_gate, i), _SideCast(ffn_w_up, i),
                _SideCast(ffn_w_down, i, 1.0 / ALPHA), _SideCast(ple_w_gate, i),
                _SideCast(ple_w_proj, i))

    def ffn(xs, i, weights, casts=()):
        return _ffn_layer(xs, ps, *weights[:3], ln2g, ln2b, weights[3], pbg, weights[4],
                          layer=i, tb=512, casts=casts)

    xs, *ffn0_w = _gmlp_layer(xs, gm_w_in[0], gm_ln_g[0], gm_ln_b[0], gm_w_s[0], gm_b_s[0],
                              gm_w_out[0], ln1g, ln1b, layer=0, tb=512, casts=ffn_casts(0))
    xs, ml_w_in_b, ml_w_out_b, *ffn1_w = ffn(
        xs, 0, ffn0_w,
        casts=(_SideCast(ml_w_in, 0), _SideCast(ml_w_out, 0, 1.0 / ALPHA)) + ffn_casts(1))
    q, kt, v, cz, gz, gt = _mlstm_proj(xs, ml_w_in_b, ml_conv_w[0], ml_conv_b[0], ml_w_q[0],
                                       ml_w_k[0], ml_w_v[0], ml_w_gates[0], ml_b_gates[0],
                                       ml_skip[0], ml_norm_g[0], batch=batch, tb=512)
    seq3 = lambda a: a.reshape(batch, seq, a.shape[-1])
    xs, = _mlstm_cell_layer(seq3(q), kt, seq3(v), seq3(cz), seq3(gz), seq3(gt), seq3(xs),
                            ml_w_out_b, ln1g, ln1b, layer=1, nb=1, tb=256)
    xs, = ffn(xs.reshape(t, d), 1, ffn1_w)
    return xs.reshape(batch, seq, d)
```

```python
import functools
import math
from typing import NamedTuple

import jax
import jax.numpy as jnp
from jax import lax
from jax.experimental import pallas as pl
from jax.experimental.pallas import tpu as pltpu

F32 = jnp.float32
BF16 = jnp.bfloat16

CHUNK = 128
LN_EPS = 1e-5
DEPTH = 2
ALPHA = (2 * DEPTH) ** 0.25
GM_GROUPS = 8
ML_HEADS = 4
ML_CONV = 4
ML_QKV_BLOCK = 4

LANES = 128
MXU_TILE = 256
VMEM_LIMIT_BYTES = 56 * 1024 * 1024


def _dot(a, b):
    return jnp.dot(a, b, preferred_element_type=F32)


def _sigmoid(x):
    return 1.0 / (1.0 + jnp.exp(-x))


def _gelu_tanh(x):
    c = 0.7978845608028654
    hx = 0.5 * x
    return hx + hx * jnp.tanh(x * (c + (c * 0.044715) * (x * x)))


def _layer_norm(x, g, b, eps=LN_EPS):
    mu = jnp.mean(x, axis=-1, keepdims=True)
    xc = x - mu
    var = jnp.mean(xc * xc, axis=-1, keepdims=True)
    return xc * lax.rsqrt(var + eps) * g + b


def _deepnorm(x, y_over_alpha, g, b):
    return _layer_norm(x + y_over_alpha, g, b, eps=LN_EPS / ALPHA ** 2)


def _layer_norm_one_pass(x, g, b):
    inv_n = 1.0 / x.shape[-1]
    mu = jnp.sum(x, axis=-1, keepdims=True) * inv_n
    var = jnp.sum(x * x, axis=-1, keepdims=True) * inv_n - mu * mu
    return (x - mu) * lax.rsqrt(var + LN_EPS) * g + b


def _resident(shape, layer=None):
    nd = len(shape)
    if layer is None:
        return pl.BlockSpec(shape, lambda *_: (0,) * nd, pipeline_mode=pl.Buffered(1))
    return pl.BlockSpec((None,) + tuple(shape), lambda *_: (layer,) + (0,) * nd,
                        pipeline_mode=pl.Buffered(1))


class _SideCast(NamedTuple):
    w: jax.Array
    layer: int | None = None
    scale: float = 1.0


BF16_SUBLANES = 16


def _side_cast_specs(casts, steps, step_of):
    in_specs, out_specs, out_shapes = [], [], []
    for c in casts:
        k, n = c.w.shape[-2:]
        cuts = [(k // r, n // cc) for r in range(BF16_SUBLANES, k + 1, BF16_SUBLANES) if k % r == 0
                for cc in range(LANES, n + 1, LANES) if n % cc == 0
                if steps % ((k // r) * (n // cc)) == 0]
        nr, ncol = max(cuts, key=lambda rc: (rc[0] * rc[1], -rc[1]))
        per = steps // (nr * ncol)

        def cell(*g, per=per, ncol=ncol):
            i = step_of(*g) // per
            return i // ncol, i % ncol

        block = (k // nr, n // ncol)
        if c.layer is None:
            in_specs.append(pl.BlockSpec(block, cell))
        else:
            in_specs.append(pl.BlockSpec((None,) + block, lambda *g, cell=cell, l=c.layer: (l, *cell(*g))))
        out_specs.append(pl.BlockSpec(block, cell))
        out_shapes.append(jax.ShapeDtypeStruct((k, n), BF16))
    return in_specs, out_specs, out_shapes


def _with_side_casts(body, n_in, n_out, casts):
    nc = len(casts)

    def kernel(*refs):
        cast_in = refs[n_in:n_in + nc]
        cast_out = refs[n_in + nc + n_out:n_in + 2 * nc + n_out]
        for c, src, dst in zip(casts, cast_in, cast_out):
            w = src[...]
            dst[...] = (w if c.scale == 1.0 else w * c.scale).astype(BF16)
        body(*refs[:n_in], *refs[n_in + nc:n_in + nc + n_out], *refs[n_in + 2 * nc + n_out:])

    return kernel


def _params(semantics):
    return pltpu.CompilerParams(dimension_semantics=semantics,
                                vmem_limit_bytes=VMEM_LIMIT_BYTES)


def _gmlp_kernel(x_ref, wi_ref, lng_ref, lnb_ref, ws_ref, bs_ref, wo_ref,
                 g1_ref, b1_ref, o_ref, vn_ref, y_ref, *, tb, half, layer):
    gdim = half // GM_GROUPS
    x = x_ref[...]
    xb = x.astype(BF16)
    pair = 2 * gdim
    u_pair = lambda j: _gelu_tanh(_dot(xb, wi_ref[:, j * pair:(j + 1) * pair]))
    v = _gelu_tanh(_dot(xb, wi_ref[:, half:]))
    u_next = u_pair(0)
    vn_ref[...] = _layer_norm_one_pass(v, lng_ref[...], lnb_ref[...]).astype(BF16)

    row = lax.broadcasted_iota(jnp.int32, (CHUNK, CHUNK), 0)
    col = lax.broadcasted_iota(jnp.int32, (CHUNK, CHUNK), 1)
    causal = col <= row
    bs = bs_ref[...]
    for j in range(GM_GROUPS // 2):
        u2 = u_next if j == 0 else u_pair(j)
        for gg in range(2):
            g = 2 * j + gg
            wsg = jnp.where(causal, ws_ref[g], 0.0).astype(BF16)
            bg = bs[:, g:g + 1]
            for c in range(tb // CHUNK):
                rows = slice(c * CHUNK, (c + 1) * CHUNK)
                sv = _dot(wsg, vn_ref[rows, g * gdim:(g + 1) * gdim]) + bg
                y_ref[rows, g * gdim:(g + 1) * gdim] = (
                    u2[rows, gg * gdim:(gg + 1) * gdim] * sv).astype(BF16)
    for r in range(2):
        rows = slice(r * (tb // 2), (r + 1) * (tb // 2))
        y = _dot(y_ref[rows, :], wo_ref[...])
        o_ref[rows, :] = _deepnorm(x_ref[rows, :], y, g1_ref[layer:layer + 1, :],
                                   b1_ref[layer:layer + 1, :])


def _gmlp_layer(x2d, w_in, ln_g, ln_b, w_s, b_s, w_out, g1, b1, *, layer, tb, casts=()):
    t, d = x2d.shape
    half = w_in.shape[1] // 2
    assert t % tb == 0 and tb % CHUNK == 0 and half % (GM_GROUPS * LANES) == 0
    row = lambda a: a.reshape(1, -1)
    tok = pl.BlockSpec((tb, d), lambda i: (i, 0))
    c_in, c_out, c_shapes = _side_cast_specs(casts, t // tb, lambda i: i)
    in_specs = [tok, _resident((d, 2 * half)),
                _resident((1, half)), _resident((1, half)),
                _resident(w_s.shape), _resident((CHUNK, GM_GROUPS)),
                _resident((half, d)), _resident(g1.shape), _resident(b1.shape)]
    return pl.pallas_call(
        _with_side_casts(functools.partial(_gmlp_kernel, tb=tb, half=half, layer=layer), len(in_specs), 1, casts),
        out_shape=(jax.ShapeDtypeStruct((t, d), F32), *c_shapes),
        grid=(t // tb,),
        in_specs=in_specs + c_in,
        out_specs=(tok, *c_out),
        scratch_shapes=[pltpu.VMEM((tb, half), BF16), pltpu.VMEM((tb, half), BF16)],
        compiler_params=_params(("arbitrary",)),
        name="gmlp_layer",
    )(x2d, w_in.astype(BF16), row(ln_g), row(ln_b), w_s, b_s.T,
      (w_out * (1.0 / ALPHA)).astype(BF16), g1, b1, *[c.w for c in casts])


def _ffn_kernel(x_ref, p_ref, wg_ref, wu_ref, wd_ref, g2_ref, b2_ref,
                wpg_ref, bpg_ref, wpp_ref, o_ref, *, layer):
    x = x_ref[...]
    xb = x.astype(BF16)
    dff = wg_ref.shape[1]
    cut = (dff // MXU_TILE // 2) * MXU_TILE
    y = None
    for cols in (slice(0, cut), slice(cut, dff)):
        g = _dot(xb, wg_ref[:, cols])
        u = _dot(xb, wu_ref[:, cols])
        h = (g * _sigmoid(g) * u).astype(BF16)
        part = _dot(h, wd_ref[cols, :])
        y = part if y is None else y + part
    pp = _dot(p_ref[...].astype(BF16), wpp_ref[...])
    vec = slice(layer, layer + 1)
    x2 = _deepnorm(x, y, g2_ref[vec, :], b2_ref[vec, :])
    gate = _sigmoid(_dot(x2.astype(BF16), wpg_ref[...]) + bpg_ref[vec, :])
    o_ref[...] = x2 + gate * pp


def _ffn_layer(x2d, p3d, w_gate, w_up, w_down, g2, b2, wpg, bpg, wpp, *, layer, tb, casts=()):
    t, d = x2d.shape
    dff = w_gate.shape[-1]
    dp = p3d.shape[-1]
    assert t % tb == 0
    tok = pl.BlockSpec((tb, d), lambda i: (i, 0))
    c_in, c_out, c_shapes = _side_cast_specs(casts, t // tb, lambda i: i)
    in_specs = [tok, pl.BlockSpec((None, tb, dp), lambda i: (layer, i, 0)),
                _resident((d, dff)), _resident((d, dff)), _resident((dff, d)),
                _resident(g2.shape), _resident(b2.shape),
                _resident((d, d)), _resident(bpg.shape), _resident((dp, d))]
    return pl.pallas_call(
        _with_side_casts(functools.partial(_ffn_kernel, layer=layer), len(in_specs), 1, casts),
        out_shape=(jax.ShapeDtypeStruct((t, d), F32), *c_shapes),
        grid=(t // tb,),
        in_specs=in_specs + c_in,
        out_specs=(tok, *c_out),
        compiler_params=_params(("arbitrary",)),
        name="ffn_layer",
    )(x2d, p3d, w_gate, w_up, w_down, g2, b2, wpg, bpg, wpp, *[c.w for c in casts])


HALO = 8


def _dot_nt(a, b):
    return lax.dot_general(a, b, (((1,), (1,)), ((), ())), preferred_element_type=F32)


def _mlstm_proj_kernel(x_ref, wi_ref, cw_ref, cb_ref, wqk_ref, wkt_ref, wv_ref, wg_ref, bg_ref,
                       skip_ref, ng_ref, q_ref, kt_ref, v_ref, cz_ref, gz_ref, gt_ref, ext_ref,
                       fc_ref, fm_ref, *, tb, inner):
    ntile = inner // MXU_TILE

    @pl.when((pl.program_id(0) == 0) & (pl.program_id(1) == 0))
    def _():
        for g in range(ntile):
            sl = slice(g * MXU_TILE, (g + 1) * MXU_TILE)
            wg_q = wg_ref[g * MXU_TILE:(g + 1) * MXU_TILE, :]
            wg_k = wg_ref[inner + g * MXU_TILE:inner + (g + 1) * MXU_TILE, :]
            wg_v = wg_ref[2 * inner + g * MXU_TILE:2 * inner + (g + 1) * MXU_TILE, :]
            fc_ref[sl, :] = (_dot(wqk_ref[g, :, :MXU_TILE], wg_q)
                             + _dot(wqk_ref[g, :, MXU_TILE:], wg_k)).astype(BF16)
            fm_ref[sl, :] = _dot(wv_ref[g], wg_v).astype(BF16)

    @pl.when(pl.program_id(1) == 0)
    def _():
        ext_ref[0:HALO, :] = jnp.zeros((HALO, inner), F32)

    xb = x_ref[...].astype(BF16)
    xm = _dot(xb, wi_ref[:, :inner])
    ext_ref[HALO:HALO + tb, :] = xm
    z = _dot(xb, wi_ref[:, inner:])
    sz = z * _sigmoid(z)
    gz_ref[...] = (ng_ref[...] * sz).astype(BF16)
    cw = cw_ref[...]
    conv = cw[ML_CONV - 1:ML_CONV, :] * xm + cb_ref[...]
    for j in range(1, ML_CONV):
        conv = conv + cw[ML_CONV - 1 - j:ML_CONV - j, :] * ext_ref[HALO - j:HALO - j + tb, :]
    ext_ref[0:HALO, :] = ext_ref[tb:tb + HALO, :]
    xc = conv * _sigmoid(conv)
    xcb = xc.astype(BF16)
    xmb = xm.astype(BF16)
    cz_ref[...] = (skip_ref[...] * xc * sz).astype(BF16)

    gates = jnp.broadcast_to(bg_ref[...], (tb, LANES))
    for g in range(ntile):
        sl = slice(g * MXU_TILE, (g + 1) * MXU_TILE)
        q_ref[:, sl] = _dot(xcb[:, sl], wqk_ref[g, :, :MXU_TILE]).astype(BF16)
        kt_ref[sl, :] = _dot_nt(wkt_ref[g], xcb[:, sl]).astype(BF16)
        v_ref[:, sl] = _dot(xmb[:, sl], wv_ref[g]).astype(BF16)
        gates = gates + _dot(xcb[:, sl], fc_ref[sl, :]) + _dot(xmb[:, sl], fm_ref[sl, :])
    gt_ref[...] = gates


def _block_diag_tiles(w, tile):
    nb, c, d = w.shape
    assert c == d and tile % c == 0 and nb % (tile // c) == 0
    cols = lax.broadcasted_iota(jnp.int32, (d, tile), 1)
    expand = (cols % d == lax.broadcasted_iota(jnp.int32, (d, tile), 0)).astype(F32)
    tiled = jnp.dot(w.reshape(nb * c, d), expand, precision=lax.Precision.HIGHEST)
    r = lax.broadcasted_iota(jnp.int32, (tile, tile), 0) // c
    s = lax.broadcasted_iota(jnp.int32, (tile, tile), 1) // d
    return jnp.where(r == s, tiled.reshape(nb * c // tile, tile, tile), 0.0).astype(BF16)


def _mlstm_proj(x2d, w_in, conv_w, conv_b, w_q, w_k, w_v, w_gates, b_gates, skip, norm_g,
                *, batch, tb):
    t, d = x2d.shape
    inner = w_in.shape[1] // 2
    seq = t // batch
    assert seq % tb == 0 and inner % MXU_TILE == 0 and tb >= HALO
    nblk = seq // tb
    ntile = inner // MXU_TILE
    wqk = jnp.concatenate([_block_diag_tiles(w_q, MXU_TILE), _block_diag_tiles(w_k, MXU_TILE)],
                          axis=-1)
    wkt = _block_diag_tiles(jnp.swapaxes(w_k, 1, 2), MXU_TILE)
    wv = _block_diag_tiles(w_v, MXU_TILE)
    ng = w_gates.shape[1]
    wg = jnp.pad(w_gates, ((0, 0), (0, LANES - ng))).astype(BF16)
    bg = jnp.pad(b_gates, (0, LANES - ng)).reshape(1, LANES)
    tok = lambda w: pl.BlockSpec((tb, w), lambda b, j: (b * nblk + j, 0))
    tok_t = pl.BlockSpec((None, inner, tb), lambda b, j: (b, 0, j))
    act = jax.ShapeDtypeStruct((t, inner), BF16)
    act_t = jax.ShapeDtypeStruct((batch, inner, seq), BF16)
    return pl.pallas_call(
        functools.partial(_mlstm_proj_kernel, tb=tb, inner=inner),
        out_shape=(act, act_t, act, act, act, jax.ShapeDtypeStruct((t, LANES), F32)),
        grid=(batch, nblk),
        in_specs=[tok(d), _resident((d, 2 * inner)),
                  _resident((ML_CONV, inner)), _resident((1, inner)),
                  _resident((ntile, MXU_TILE, 2 * MXU_TILE)), _resident((ntile, MXU_TILE, MXU_TILE)),
                  _resident((ntile, MXU_TILE, MXU_TILE)),
                  _resident((3 * inner, LANES)), _resident((1, LANES)),
                  _resident((1, inner)), _resident((1, inner))],
        out_specs=(tok(inner), tok_t, tok(inner), tok(inner), tok(inner), tok(LANES)),
        scratch_shapes=[pltpu.VMEM((tb + HALO, inner), F32),
                        pltpu.VMEM((inner, LANES), BF16),
                        pltpu.VMEM((inner, LANES), BF16)],
        compiler_params=_params(("arbitrary", "arbitrary")),
        name="mlstm_proj",
    )(x2d, w_in, conv_w, conv_b.reshape(1, inner), wqk, wkt, wv, wg, bg,
      skip.reshape(1, inner), norm_g.reshape(1, inner))


def _split3(a):
    hi = a.astype(BF16)
    r1 = a - hi.astype(F32)
    mid = r1.astype(BF16)
    lo = (r1 - mid.astype(F32)).astype(BF16)
    return hi, mid, lo


CELL_CHUNK = 256

def _mlstm_cell_kernel(q_ref, kt_ref, v_ref, cz_ref, gz_ref, gt_ref, x_ref,
                       wo_ref, g1_ref, b1_ref, o_ref,
                       c_ref, n_ref, m_ref, hg_ref, cb_ref, *, nb, tb, dh, lc, layer):
    nh = ML_HEADS
    scale = dh ** -0.5
    log_scale = -0.5 * math.log(dh)

    @pl.when(pl.program_id(1) == 0)
    def _():
        for ref in (c_ref, n_ref, m_ref):
            ref[...] = jnp.zeros(ref.shape, F32)

    row = lax.broadcasted_iota(jnp.int32, (lc, lc), 0)
    col = lax.broadcasted_iota(jnp.int32, (lc, lc), 1)
    causal = col <= row
    tri = jnp.where(causal, 1.0, 0.0).astype(BF16)
    lane = lax.broadcasted_iota(jnp.int32, (lc, LANES), 1)

    for c in range(tb // lc):
        rows = slice(c * lc, (c + 1) * lc)
        for b in range(nb):
            gt = gt_ref[b, rows, :]
            lf = jnp.minimum(gt, 0.0) - jnp.log(1.0 + jnp.exp(-jnp.abs(gt)))
            hi, mid, lo = _split3(lf)
            bcum = _dot(tri, hi) + _dot(tri, mid) + _dot(tri, lo)
            colmat = jnp.where(lane < nh, gt, bcum)
            rowmat = colmat.T

            def early(h):
                hs = slice(h * dh, (h + 1) * dh)
                hd = dict(st=b * nh + h, hs=hs,
                          i_row=rowmat[h:h + 1, :], b_row=rowmat[nh + h:nh + h + 1, :],
                          b_col=colmat[:, nh + h:nh + h + 1],
                          q=q_ref[b, rows, hs], kt=kt_ref[b, hs, rows], v=v_ref[b, rows, hs])
                st = hd["st"]
                hd["m_prev"] = m_ref[st:st + 1, 0:1]
                hd["n_prev"] = n_ref[st]
                cb_ref[st] = c_ref[st].astype(BF16)
                hd["qk"] = _dot(hd["q"], hd["kt"])
                return hd

            heads = [early(h) for h in range(nh)]

            def update(hd):
                b_last = hd["b_row"][:, lc - 1:lc]
                g_row = b_last - hd["b_row"] + hd["i_row"]
                m_new = jnp.maximum(b_last + hd["m_prev"], jnp.max(g_row, axis=-1, keepdims=True))
                wg_row = jnp.exp(g_row - m_new)
                hd["decay"] = jnp.exp(b_last + hd["m_prev"] - m_new)
                hd["m_new"] = m_new
                kw = (hd["kt"].astype(F32) * wg_row).astype(BF16)
                hd["c_upd"] = _dot(kw, hd["v"])
                hd["n_upd"] = _dot_nt(jnp.ones((8, lc), BF16), kw)

            def output(hd):
                st, hs = hd["st"], hd["hs"]
                dmat = jnp.where(causal, hd["b_col"] - hd["b_row"] + hd["i_row"], -jnp.inf)
                m_inter = hd["b_col"] + hd["m_prev"]
                m_t = jnp.maximum(m_inter, jnp.max(dmat, axis=-1, keepdims=True))
                s = hd["qk"] * jnp.exp(dmat - (m_t - log_scale))
                si = jnp.exp(m_inter - m_t) * scale
                qs = hd["q"].astype(F32) * si
                num = _dot(s.astype(BF16), hd["v"]) + _dot(qs.astype(BF16), cb_ref[st])
                den = (jnp.sum(s, axis=-1, keepdims=True)
                       + jnp.sum(qs * hd["n_prev"], axis=-1, keepdims=True))
                r = 1.0 / jnp.maximum(jnp.abs(den), jnp.exp(-m_t))
                nc = num - jnp.mean(num, axis=-1, keepdims=True)
                var = jnp.mean(nc * nc, axis=-1, keepdims=True)
                hn = nc * (r * lax.rsqrt(r * r * var + LN_EPS))
                hg = hn * gz_ref[b, rows, hs].astype(F32) + cz_ref[b, rows, hs].astype(F32)
                hg_ref[b, rows, hs] = hg.astype(BF16)

            def advance(hd):
                st = hd["st"]
                c_ref[st] = hd["decay"] * c_ref[st] + hd["c_upd"]
                n_ref[st] = hd["decay"] * hd["n_prev"] + hd["n_upd"][0:1, :]
                m_ref[st:st + 1, :] = jnp.broadcast_to(hd["m_new"], (1, LANES))

            for hd in heads:
                update(hd)
            for hd in heads:
                advance(hd)
            for hd in heads:
                output(hd)

            y = _dot(hg_ref[b, rows, :], wo_ref[...])
            o_ref[b, rows, :] = _deepnorm(x_ref[b, rows, :], y, g1_ref[layer:layer + 1, :],
                                          b1_ref[layer:layer + 1, :])


def _mlstm_cell_layer(q, kt, v, cz, gz, gt, x3d, w_out, g1, b1, *, layer, nb, tb, casts=()):
    batch, seq, inner = q.shape
    d = x3d.shape[-1]
    dh = inner // ML_HEADS
    assert seq % tb == 0 and tb % CELL_CHUNK == 0 and batch % nb == 0
    tok = lambda w: pl.BlockSpec((nb, tb, w), lambda bi, j: (bi, j, 0))
    tok_t = pl.BlockSpec((nb, inner, tb), lambda bi, j: (bi, 0, j))
    nstate = nb * ML_HEADS
    nblk = seq // tb
    c_in, c_out, c_shapes = _side_cast_specs(casts, (batch // nb) * nblk, lambda bi, j: bi * nblk + j)
    in_specs = [tok(inner), tok_t, tok(inner), tok(inner), tok(inner), tok(LANES), tok(d),
                _resident((inner, d)), _resident(g1.shape), _resident(b1.shape)]
    return pl.pallas_call(
        _with_side_casts(functools.partial(_mlstm_cell_kernel, nb=nb, tb=tb, dh=dh, lc=CELL_CHUNK, layer=layer),
                         len(in_specs), 1, casts),
        out_shape=(jax.ShapeDtypeStruct((batch, seq, d), F32), *c_shapes),
        grid=(batch // nb, nblk),
        in_specs=in_specs + c_in,
        out_specs=(tok(d), *c_out),
        scratch_shapes=[pltpu.VMEM((nstate, dh, dh), F32),
                        pltpu.VMEM((nstate, 1, dh), F32),
                        pltpu.VMEM((-(-nstate // 8) * 8, LANES), F32),
                        pltpu.VMEM((nb, tb, inner), BF16),
                        pltpu.VMEM((nstate, dh, dh), BF16)],
        compiler_params=_params(("arbitrary", "arbitrary")),
        name="mlstm_cell",
    )(q, kt, v, cz, gz, gt, x3d, w_out, g1, b1, *[c.w for c in casts])


def kernel(x, p, gm_w_in, gm_ln_g, gm_ln_b, gm_w_s, gm_b_s, gm_w_out, ml_w_in, ml_conv_w, ml_conv_b, ml_w_q, ml_w_k, ml_w_v, ml_w_gates, ml_b_gates, ml_skip, ml_norm_g, ml_w_out, ln1_g, ln1_b, ln2_g, ln2_b, ffn_w_gate, ffn_w_up, ffn_w_down, ple_w_proj, ple_w_gate, ple_b_gate):
    batch, seq, d = x.shape
    t = batch * seq
    xs = x.reshape(t, d)
    ps = p.reshape(p.shape[0], t, p.shape[-1])
    ln1g, ln1b, ln2g, ln2b, pbg = ln1_g, ln1_b, ln2_g, ln2_b, ple_b_gate

    def ffn_casts(i):
        return (_SideCast(ffn_w_gate, i), _SideCast(ffn_w_up, i),
                _SideCast(ffn_w_down, i, 1.0 / ALPHA), _SideCast(ple_w_gate, i),
                _SideCast(ple_w_proj, i))

    def ffn(xs, i, weights, casts=()):
        return _ffn_layer(xs, ps, *weights[:3], ln2g, ln2b, weights[3], pbg, weights[4],
                          layer=i, tb=512, casts=casts)

    xs, *ffn0_w = _gmlp_layer(xs, gm_w_in[0], gm_ln_g[0], gm_ln_b[0], gm_w_s[0], gm_b_s[0],
                              gm_w_out[0], ln1g, ln1b, layer=0, tb=512, casts=ffn_casts(0))
    xs, ml_w_in_b, ml_w_out_b, *ffn1_w = ffn(
        xs, 0, ffn0_w,
        casts=(_SideCast(ml_w_in, 0), _SideCast(ml_w_out, 0, 1.0 / ALPHA)) + ffn_casts(1))
    q, kt, v, cz, gz, gt = _mlstm_proj(xs, ml_w_in_b, ml_conv_w[0], ml_conv_b[0], ml_w_q[0],
                                       ml_w_k[0], ml_w_v[0], ml_w_gates[0], ml_b_gates[0],
                                       ml_skip[0], ml_norm_g[0], batch=batch, tb=512)
    seq3 = lambda a: a.reshape(batch, seq, a.shape[-1])
    xs, = _mlstm_cell_layer(seq3(q), kt, seq3(v), seq3(cz), seq3(gz), seq3(gt), seq3(xs),
                            ml_w_out_b, ln1g, ln1b, layer=1, nb=1, tb=256)
    xs, = ffn(xs.reshape(t, d), 1, ffn1_w)
    return xs.reshape(batch, seq, d)
```

```python
import functools
import math
from typing import NamedTuple

import jax
import jax.numpy as jnp
from jax import lax
from jax.experimental import pallas as pl
from jax.experimental.pallas import tpu as pltpu

F32 = jnp.float32
BF16 = jnp.bfloat16

CHUNK = 128
LN_EPS = 1e-5
DEPTH = 2
ALPHA = (2 * DEPTH) ** 0.25
GM_GROUPS = 8
ML_HEADS = 4
ML_CONV = 4
ML_QKV_BLOCK = 4

LANES = 128
MXU_TILE = 256
VMEM_LIMIT_BYTES = 56 * 1024 * 1024


def _dot(a, b):
    return jnp.dot(a, b, preferred_element_type=F32)


def _sigmoid(x):
    return 1.0 / (1.0 + jnp.exp(-x))


def _gelu_tanh(x):
    c = 0.7978845608028654
    hx = 0.5 * x
    return hx + hx * jnp.tanh(x * (c + (c * 0.044715) * (x * x)))


def _layer_norm(x, g, b, eps=LN_EPS):
    mu = jnp.mean(x, axis=-1, keepdims=True)
    xc = x - mu
    var = jnp.mean(xc * xc, axis=-1, keepdims=True)
    return xc * lax.rsqrt(var + eps) * g + b


def _deepnorm(x, y_over_alpha, g, b):
    return _layer_norm(x + y_over_alpha, g, b, eps=LN_EPS / ALPHA ** 2)


def _layer_norm_one_pass(x, g, b):
    inv_n = 1.0 / x.shape[-1]
    mu = jnp.sum(x, axis=-1, keepdims=True) * inv_n
    var = jnp.sum(x * x, axis=-1, keepdims=True) * inv_n - mu * mu
    return (x - mu) * lax.rsqrt(var + LN_EPS) * g + b


def _resident(shape, layer=None):
    nd = len(shape)
    if layer is None:
        return pl.BlockSpec(shape, lambda *_: (0,) * nd, pipeline_mode=pl.Buffered(1))
    return pl.BlockSpec((None,) + tuple(shape), lambda *_: (layer,) + (0,) * nd,
                        pipeline_mode=pl.Buffered(1))


class _SideCast(NamedTuple):
    w: jax.Array
    layer: int | None = None
    scale: float = 1.0


BF16_SUBLANES = 16


def _side_cast_specs(casts, steps, step_of):
    in_specs, out_specs, out_shapes = [], [], []
    for c in casts:
        k, n = c.w.shape[-2:]
        cuts = [(k // r, n // cc) for r in range(BF16_SUBLANES, k + 1, BF16_SUBLANES) if k % r == 0
                for cc in range(LANES, n + 1, LANES) if n % cc == 0
                if steps % ((k // r) * (n // cc)) == 0]
        nr, ncol = max(cuts, key=lambda rc: (rc[0] * rc[1], -rc[1]))
        per = steps // (nr * ncol)

        def cell(*g, per=per, ncol=ncol):
            i = step_of(*g) // per
            return i // ncol, i % ncol

        block = (k // nr, n // ncol)
        if c.layer is None:
            in_specs.append(pl.BlockSpec(block, cell))
        else:
            in_specs.append(pl.BlockSpec((None,) + block, lambda *g, cell=cell, l=c.layer: (l, *cell(*g))))
        out_specs.append(pl.BlockSpec(block, cell))
        out_shapes.append(jax.ShapeDtypeStruct((k, n), BF16))
    return in_specs, out_specs, out_shapes


def _with_side_casts(body, n_in, n_out, casts):
    nc = len(casts)

    def kernel(*refs):
        cast_in = refs[n_in:n_in + nc]
        cast_out = refs[n_in + nc + n_out:n_in + 2 * nc + n_out]
        for c, src, dst in zip(casts, cast_in, cast_out):
            w = src[...]
            dst[...] = (w if c.scale == 1.0 else w * c.scale).astype(BF16)
        body(*refs[:n_in], *refs[n_in + nc:n_in + nc + n_out], *refs[n_in + 2 * nc + n_out:])

    return kernel


def _params(semantics):
    return pltpu.CompilerParams(dimension_semantics=semantics,
                                vmem_limit_bytes=VMEM_LIMIT_BYTES)


def _gmlp_kernel(x_ref, wi_ref, lng_ref, lnb_ref, ws_ref, bs_ref, wo_ref,
                 g1_ref, b1_ref, o_ref, vn_ref, y_ref, *, tb, half, layer):
    gdim = half // GM_GROUPS
    x = x_ref[...]
    xb = x.astype(BF16)
    pair = 2 * gdim
    u_pair = lambda j: _gelu_tanh(_dot(xb, wi_ref[:, j * pair:(j + 1) * pair]))
    v = _gelu_tanh(_dot(xb, wi_ref[:, half:]))
    u_next = u_pair(0)
    vn_ref[...] = _layer_norm_one_pass(v, lng_ref[...], lnb_ref[...]).astype(BF16)

    row = lax.broadcasted_iota(jnp.int32, (CHUNK, CHUNK), 0)
    col = lax.broadcasted_iota(jnp.int32, (CHUNK, CHUNK), 1)
    causal = col <= row
    bs = bs_ref[...]
    for j in range(GM_GROUPS // 2):
        u2 = u_next if j == 0 else u_pair(j)
        for gg in range(2):
            g = 2 * j + gg
            wsg = jnp.where(causal, ws_ref[g], 0.0).astype(BF16)
            bg = bs[:, g:g + 1]
            for c in range(tb // CHUNK):
                rows = slice(c * CHUNK, (c + 1) * CHUNK)
                sv = _dot(wsg, vn_ref[rows, g * gdim:(g + 1) * gdim]) + bg
                y_ref[rows, g * gdim:(g + 1) * gdim] = (
                    u2[rows, gg * gdim:(gg + 1) * gdim] * sv).astype(BF16)
    for r in range(2):
        rows = slice(r * (tb // 2), (r + 1) * (tb // 2))
        y = _dot(y_ref[rows, :], wo_ref[...])
        o_ref[rows, :] = _deepnorm(x_ref[rows, :], y, g1_ref[layer:layer + 1, :],
                                   b1_ref[layer:layer + 1, :])


def _gmlp_layer(x2d, w_in, ln_g, ln_b, w_s, b_s, w_out, g1, b1, *, layer, tb, casts=()):
    t, d = x2d.shape
    half = w_in.shape[1] // 2
    assert t % tb == 0 and tb % CHUNK == 0 and half % (GM_GROUPS * LANES) == 0
    row = lambda a: a.reshape(1, -1)
    tok = pl.BlockSpec((tb, d), lambda i: (i, 0))
    c_in, c_out, c_shapes = _side_cast_specs(casts, t // tb, lambda i: i)
    in_specs = [tok, _resident((d, 2 * half)),
                _resident((1, half)), _resident((1, half)),
                _resident(w_s.shape), _resident((CHUNK, GM_GROUPS)),
                _resident((half, d)), _resident(g1.shape), _resident(b1.shape)]
    return pl.pallas_call(
        _with_side_casts(functools.partial(_gmlp_kernel, tb=tb, half=half, layer=layer), len(in_specs), 1, casts),
        out_shape=(jax.ShapeDtypeStruct((t, d), F32), *c_shapes),
        grid=(t // tb,),
        in_specs=in_specs + c_in,
        out_specs=(tok, *c_out),
        scratch_shapes=[pltpu.VMEM((tb, half), BF16), pltpu.VMEM((tb, half), BF16)],
        compiler_params=_params(("arbitrary",)),
        name="gmlp_layer",
    )(x2d, w_in.astype(BF16), row(ln_g), row(ln_b), w_s, b_s.T,
      (w_out * (1.0 / ALPHA)).astype(BF16), g1, b1, *[c.w for c in casts])


def _ffn_kernel(x_ref, p_ref, wg_ref, wu_ref, wd_ref, g2_ref, b2_ref,
                wpg_ref, bpg_ref, wpp_ref, o_ref, *, layer):
    x = x_ref[...]
    xb = x.astype(BF16)
    dff = wg_ref.shape[1]
    cut = (dff // MXU_TILE // 2) * MXU_TILE
    y = None
    for cols in (slice(0, cut), slice(cut, dff)):
        g = _dot(xb, wg_ref[:, cols])
        u = _dot(xb, wu_ref[:, cols])
        h = (g * _sigmoid(g) * u).astype(BF16)
        part = _dot(h, wd_ref[cols, :])
        y = part if y is None else y + part
    pp = _dot(p_ref[...].astype(BF16), wpp_ref[...])
    vec = slice(layer, layer + 1)
    x2 = _deepnorm(x, y, g2_ref[vec, :], b2_ref[vec, :])
    gate = _sigmoid(_dot(x2.astype(BF16), wpg_ref[...]) + bpg_ref[vec, :])
    o_ref[...] = x2 + gate * pp


def _ffn_layer(x2d, p3d, w_gate, w_up, w_down, g2, b2, wpg, bpg, wpp, *, layer, tb, casts=()):
    t, d = x2d.shape
    dff = w_gate.shape[-1]
    dp = p3d.shape[-1]
    assert t % tb == 0
    tok = pl.BlockSpec((tb, d), lambda i: (i, 0))
    c_in, c_out, c_shapes = _side_cast_specs(casts, t // tb, lambda i: i)
    in_specs = [tok, pl.BlockSpec((None, tb, dp), lambda i: (layer, i, 0)),
                _resident((d, dff)), _resident((d, dff)), _resident((dff, d)),
                _resident(g2.shape), _resident(b2.shape),
                _resident((d, d)), _resident(bpg.shape), _resident((dp, d))]
    return pl.pallas_call(
        _with_side_casts(functools.partial(_ffn_kernel, layer=layer), len(in_specs), 1, casts),
        out_shape=(jax.ShapeDtypeStruct((t, d), F32), *c_shapes),
        grid=(t // tb,),
        in_specs=in_specs + c_in,
        out_specs=(tok, *c_out),
        compiler_params=_params(("arbitrary",)),
        name="ffn_layer",
    )(x2d, p3d, w_gate, w_up, w_down, g2, b2, wpg, bpg, wpp, *[c.w for c in casts])


HALO = 8


def _dot_nt(a, b):
    return lax.dot_general(a, b, (((1,), (1,)), ((), ())), preferred_element_type=F32)


def _mlstm_proj_kernel(x_ref, wi_ref, cw_ref, cb_ref, wqk_ref, wkt_ref, wv_ref, wg_ref, bg_ref,
                       skip_ref, ng_ref, q_ref, kt_ref, v_ref, cz_ref, gz_ref, gt_ref, ext_ref,
                       fc_ref, fm_ref, *, tb, inner):
    ntile = inner // MXU_TILE

    @pl.when((pl.program_id(0) == 0) & (pl.program_id(1) == 0))
    def _():
        for g in range(ntile):
            sl = slice(g * MXU_TILE, (g + 1) * MXU_TILE)
            wg_q = wg_ref[g * MXU_TILE:(g + 1) * MXU_TILE, :]
            wg_k = wg_ref[inner + g * MXU_TILE:inner + (g + 1) * MXU_TILE, :]
            wg_v = wg_ref[2 * inner + g * MXU_TILE:2 * inner + (g + 1) * MXU_TILE, :]
            fc_ref[sl, :] = (_dot(wqk_ref[g, :, :MXU_TILE], wg_q)
                             + _dot(wqk_ref[g, :, MXU_TILE:], wg_k)).astype(BF16)
            fm_ref[sl, :] = _dot(wv_ref[g], wg_v).astype(BF16)

    @pl.when(pl.program_id(1) == 0)
    def _():
        ext_ref[0:HALO, :] = jnp.zeros((HALO, inner), F32)

    xb = x_ref[...].astype(BF16)
    xm = _dot(xb, wi_ref[:, :inner])
    ext_ref[HALO:HALO + tb, :] = xm
    z = _dot(xb, wi_ref[:, inner:])
    sz = z * _sigmoid(z)
    gz_ref[...] = (ng_ref[...] * sz).astype(BF16)
    cw = cw_ref[...]
    conv = cw[ML_CONV - 1:ML_CONV, :] * xm + cb_ref[...]
    for j in range(1, ML_CONV):
        conv = conv + cw[ML_CONV - 1 - j:ML_CONV - j, :] * ext_ref[HALO - j:HALO - j + tb, :]
    ext_ref[0:HALO, :] = ext_ref[tb:tb + HALO, :]
    xc = conv * _sigmoid(conv)
    xcb = xc.astype(BF16)
    xmb = xm.astype(BF16)
    cz_ref[...] = (skip_ref[...] * xc * sz).astype(BF16)

    gates = jnp.broadcast_to(bg_ref[...], (tb, LANES))
    for g in range(ntile):
        sl = slice(g * MXU_TILE, (g + 1) * MXU_TILE)
        q_ref[:, sl] = _dot(xcb[:, sl], wqk_ref[g, :, :MXU_TILE]).astype(BF16)
        kt_ref[sl, :] = _dot_nt(wkt_ref[g], xcb[:, sl]).astype(BF16)
        v_ref[:, sl] = _dot(xmb[:, sl], wv_ref[g]).astype(BF16)
        gates = gates + _dot(xcb[:, sl], fc_ref[sl, :]) + _dot(xmb[:, sl], fm_ref[sl, :])
    gt_ref[...] = gates


def _block_diag_tiles(w, tile):
    nb, c, d = w.shape
    assert c == d and tile % c == 0 and nb % (tile // c) == 0
    cols = lax.broadcasted_iota(jnp.int32, (d, tile), 1)
    expand = (cols % d == lax.broadcasted_iota(jnp.int32, (d, tile), 0)).astype(F32)
    tiled = jnp.dot(w.reshape(nb * c, d), expand, precision=lax.Precision.HIGHEST)
    r = lax.broadcasted_iota(jnp.int32, (tile, tile), 0) // c
    s = lax.broadcasted_iota(jnp.int32, (tile, tile), 1) // d
    return jnp.where(r == s, tiled.reshape(nb * c // tile, tile, tile), 0.0).astype(BF16)


def _mlstm_proj(x2d, w_in, conv_w, conv_b, w_q, w_k, w_v, w_gates, b_gates, skip, norm_g,
                *, batch, tb):
    t, d = x2d.shape
    inner = w_in.shape[1] // 2
    seq = t // batch
    assert seq % tb == 0 and inner % MXU_TILE == 0 and tb >= HALO
    nblk = seq // tb
    ntile = inner // MXU_TILE
    wqk = jnp.concatenate([_block_diag_tiles(w_q, MXU_TILE), _block_diag_tiles(w_k, MXU_TILE)],
                          axis=-1)
    wkt = _block_diag_tiles(jnp.swapaxes(w_k, 1, 2), MXU_TILE)
    wv = _block_diag_tiles(w_v, MXU_TILE)
    ng = w_gates.shape[1]
    wg = jnp.pad(w_gates, ((0, 0), (0, LANES - ng))).astype(BF16)
    bg = jnp.pad(b_gates, (0, LANES - ng)).reshape(1, LANES)
    tok = lambda w: pl.BlockSpec((tb, w), lambda b, j: (b * nblk + j, 0))
    tok_t = pl.BlockSpec((None, inner, tb), lambda b, j: (b, 0, j))
    act = jax.ShapeDtypeStruct((t, inner), BF16)
    act_t = jax.ShapeDtypeStruct((batch, inner, seq), BF16)
    return pl.pallas_call(
        functools.partial(_mlstm_proj_kernel, tb=tb, inner=inner),
        out_shape=(act, act_t, act, act, act, jax.ShapeDtypeStruct((t, LANES), F32)),
        grid=(batch, nblk),
        in_specs=[tok(d), _resident((d, 2 * inner)),
                  _resident((ML_CONV, inner)), _resident((1, inner)),
                  _resident((ntile, MXU_TILE, 2 * MXU_TILE)), _resident((ntile, MXU_TILE, MXU_TILE)),
                  _resident((ntile, MXU_TILE, MXU_TILE)),
                  _resident((3 * inner, LANES)), _resident((1, LANES)),
                  _resident((1, inner)), _resident((1, inner))],
        out_specs=(tok(inner), tok_t, tok(inner), tok(inner), tok(inner), tok(LANES)),
        scratch_shapes=[pltpu.VMEM((tb + HALO, inner), F32),
                        pltpu.VMEM((inner, LANES), BF16),
                        pltpu.VMEM((inner, LANES), BF16)],
        compiler_params=_params(("arbitrary", "arbitrary")),
        name="mlstm_proj",
    )(x2d, w_in, conv_w, conv_b.reshape(1, inner), wqk, wkt, wv, wg, bg,
      skip.reshape(1, inner), norm_g.reshape(1, inner))


def _split3(a):
    hi = a.astype(BF16)
    r1 = a - hi.astype(F32)
    mid = r1.astype(BF16)
    lo = (r1 - mid.astype(F32)).astype(BF16)
    return hi, mid, lo


CELL_CHUNK = 256

def _mlstm_cell_kernel(q_ref, kt_ref, v_ref, cz_ref, gz_ref, gt_ref, x_ref,
                       wo_ref, g1_ref, b1_ref, o_ref,
                       c_ref, n_ref, m_ref, hg_ref, cb_ref, *, nb, tb, dh, lc, layer):
    nh = ML_HEADS
    scale = dh ** -0.5
    log_scale = -0.5 * math.log(dh)

    @pl.when(pl.program_id(1) == 0)
    def _():
        for ref in (c_ref, n_ref, m_ref):
            ref[...] = jnp.zeros(ref.shape, F32)

    row = lax.broadcasted_iota(jnp.int32, (lc, lc), 0)
    col = lax.broadcasted_iota(jnp.int32, (lc, lc), 1)
    causal = col <= row
    tri = jnp.where(causal, 1.0, 0.0).astype(BF16)
    lane = lax.broadcasted_iota(jnp.int32, (lc, LANES), 1)

    for c in range(tb // lc):
        rows = slice(c * lc, (c + 1) * lc)
        for b in range(nb):
            gt = gt_ref[b, rows, :]
            lf = jnp.minimum(gt, 0.0) - jnp.log(1.0 + jnp.exp(-jnp.abs(gt)))
            hi, mid, lo = _split3(lf)
            bcum = _dot(tri, hi) + _dot(tri, mid) + _dot(tri, lo)
            colmat = jnp.where(lane < nh, gt, bcum)
            rowmat = colmat.T

            def early(h):
                hs = slice(h * dh, (h + 1) * dh)
                hd = dict(st=b * nh + h, hs=hs,
                          i_row=rowmat[h:h + 1, :], b_row=rowmat[nh + h:nh + h + 1, :],
                          b_col=colmat[:, nh + h:nh + h + 1],
                          q=q_ref[b, rows, hs], kt=kt_ref[b, hs, rows], v=v_ref[b, rows, hs])
                st = hd["st"]
                hd["m_prev"] = m_ref[st:st + 1, 0:1]
                hd["n_prev"] = n_ref[st]
                cb_ref[st] = c_ref[st].astype(BF16)
                hd["qk"] = _dot(hd["q"], hd["kt"])
                return hd

            heads = [early(h) for h in range(nh)]

            def update(hd):
                b_last = hd["b_row"][:, lc - 1:lc]
                g_row = b_last - hd["b_row"] + hd["i_row"]
                m_new = jnp.maximum(b_last + hd["m_prev"], jnp.max(g_row, axis=-1, keepdims=True))
                wg_row = jnp.exp(g_row - m_new)
                hd["decay"] = jnp.exp(b_last + hd["m_prev"] - m_new)
                hd["m_new"] = m_new
                kw = (hd["kt"].astype(F32) * wg_row).astype(BF16)
                hd["c_upd"] = _dot(kw, hd["v"])
                hd["n_upd"] = _dot_nt(jnp.ones((8, lc), BF16), kw)

            def output(hd):
                st, hs = hd["st"], hd["hs"]
                dmat = jnp.where(causal, hd["b_col"] - hd["b_row"] + hd["i_row"], -jnp.inf)
                m_inter = hd["b_col"] + hd["m_prev"]
                m_t = jnp.maximum(m_inter, jnp.max(dmat, axis=-1, keepdims=True))
                s = hd["qk"] * jnp.exp(dmat - (m_t - log_scale))
                si = jnp.exp(m_inter - m_t) * scale
                qs = hd["q"].astype(F32) * si
                num = _dot(s.astype(BF16), hd["v"]) + _dot(qs.astype(BF16), cb_ref[st])
                den = (jnp.sum(s, axis=-1, keepdims=True)
                       + jnp.sum(qs * hd["n_prev"], axis=-1, keepdims=True))
                r = 1.0 / jnp.maximum(jnp.abs(den), jnp.exp(-m_t))
                nc = num - jnp.mean(num, axis=-1, keepdims=True)
                var = jnp.mean(nc * nc, axis=-1, keepdims=True)
                hn = nc * (r * lax.rsqrt(r * r * var + LN_EPS))
                hg = hn * gz_ref[b, rows, hs].astype(F32) + cz_ref[b, rows, hs].astype(F32)
                hg_ref[b, rows, hs] = hg.astype(BF16)

            def advance(hd):
                st = hd["st"]
                c_ref[st] = hd["decay"] * c_ref[st] + hd["c_upd"]
                n_ref[st] = hd["decay"] * hd["n_prev"] + hd["n_upd"][0:1, :]
                m_ref[st:st + 1, :] = jnp.broadcast_to(hd["m_new"], (1, LANES))

            for hd in heads:
                update(hd)
            for hd in heads:
                advance(hd)
            for hd in heads:
                output(hd)

            y = _dot(hg_ref[b, rows, :], wo_ref[...])
            o_ref[b, rows, :] = _deepnorm(x_ref[b, rows, :], y, g1_ref[layer:layer + 1, :],
                                          b1_ref[layer:layer + 1, :])


def _mlstm_cell_layer(q, kt, v, cz, gz, gt, x3d, w_out, g1, b1, *, layer, nb, tb, casts=()):
    batch, seq, inner = q.shape
    d = x3d.shape[-1]
    dh = inner // ML_HEADS
    assert seq % tb == 0 and tb % CELL_CHUNK == 0 and batch % nb == 0
    tok = lambda w: pl.BlockSpec((nb, tb, w), lambda bi, j: (bi, j, 0))
    tok_t = pl.BlockSpec((nb, inner, tb), lambda bi, j: (bi, 0, j))
    nstate = nb * ML_HEADS
    nblk = seq // tb
    c_in, c_out, c_shapes = _side_cast_specs(casts, (batch // nb) * nblk, lambda bi, j: bi * nblk + j)
    in_specs = [tok(inner), tok_t, tok(inner), tok(inner), tok(inner), tok(LANES), tok(d),
                _resident((inner, d)), _resident(g1.shape), _resident(b1.shape)]
    return pl.pallas_call(
        _with_side_casts(functools.partial(_mlstm_cell_kernel, nb=nb, tb=tb, dh=dh, lc=CELL_CHUNK, layer=layer),
                         len(in_specs), 1, casts),
        out_shape=(jax.ShapeDtypeStruct((batch, seq, d), F32), *c_shapes),
        grid=(batch // nb, nblk),
        in_specs=in_specs + c_in,
        out_specs=(tok(d), *c_out),
        scratch_shapes=[pltpu.VMEM((nstate, dh, dh), F32),
                        pltpu.VMEM((nstate, 1, dh), F32),
                        pltpu.VMEM((-(-nstate // 8) * 8, LANES), F32),
                        pltpu.VMEM((nb, tb, inner), BF16),
                        pltpu.VMEM((nstate, dh, dh), BF16)],
        compiler_params=_params(("arbitrary", "arbitrary")),
        name="mlstm_cell",
    )(q, kt, v, cz, gz, gt, x3d, w_out, g1, b1, *[c.w for c in casts])


def kernel(x, p, gm_w_in, gm_ln_g, gm_ln_b, gm_w_s, gm_b_s, gm_w_out, ml_w_in, ml_conv_w, ml_conv_b, ml_w_q, ml_w_k, ml_w_v, ml_w_gates, ml_b_gates, ml_skip, ml_norm_g, ml_w_out, ln1_g, ln1_b, ln2_g, ln2_b, ffn_w_gate, ffn_w_up, ffn_w_down, ple_w_proj, ple_w_gate, ple_b_gate):
    batch, seq, d = x.shape
    t = batch * seq
    xs = x.reshape(t, d)
    ps = p.reshape(p.shape[0], t, p.shape[-1])
    ln1g, ln1b, ln2g, ln2b, pbg = ln1_g, ln1_b, ln2_g, ln2_b, ple_b_gate

    def ffn_casts(i):
        return (_SideCast(ffn_w_gate, i), _SideCast(ffn_w_up, i),
                _SideCast(ffn_w_down, i, 1.0 / ALPHA), _SideCast(ple_w_gate, i),
                _SideCast(ple_w_proj, i))

    def ffn(xs, i, weights, casts=()):
        return _ffn_layer(xs, ps, *weights[:3], ln2g, ln2b, weights[3], pbg, weights[4],
                          layer=i, tb=512, casts=casts)

    xs, *ffn0_w = _gmlp_layer(xs, gm_w_in[0], gm_ln_g[0], gm_ln_b[0], gm_w_s[0], gm_b_s[0],
                              gm_w_out[0], ln1g, ln1b, layer=0, tb=512, casts=ffn_casts(0))
    xs, ml_w_in_b, ml_w_out_b, *ffn1_w = ffn(
        xs, 0, ffn0_w,
        casts=(_SideCast(ml_w_in, 0), _SideCast(ml_w_out, 0, 1.0 / ALPHA)) + ffn_casts(1))
    q, kt, v, cz, gz, gt = _mlstm_proj(xs, ml_w_in_b, ml_conv_w[0], ml_conv_b[0], ml_w_q[0],
                                       ml_w_k[0], ml_w_v[0], ml_w_gates[0], ml_b_gates[0],
                                       ml_skip[0], ml_norm_g[0], batch=batch, tb=512)
    seq3 = lambda a: a.reshape(batch, seq, a.shape[-1])
    xs, = _mlstm_cell_layer(seq3(q), kt, seq3(v), seq3(cz), seq3(gz), seq3(gt), seq3(xs),
                            ml_w_out_b, ln1g, ln1b, layer=1, nb=1, tb=512)
    xs, = ffn(xs.reshape(t, d), 1, ffn1_w)
    return xs.reshape(batch, seq, d)
```

```python
import functools
import math
from typing import NamedTuple

import jax
import jax.numpy as jnp
from jax import lax
from jax.experimental import pallas as pl
from jax.experimental.pallas import tpu as pltpu

F32 = jnp.float32
BF16 = jnp.bfloat16

CHUNK = 128
LN_EPS = 1e-5
DEPTH = 2
ALPHA = (2 * DEPTH) ** 0.25
GM_GROUPS = 8
ML_HEADS = 4
ML_CONV = 4
ML_QKV_BLOCK = 4

LANES = 128
MXU_TILE = 256
VMEM_LIMIT_BYTES = 56 * 1024 * 1024


def _dot(a, b):
    return jnp.dot(a, b, preferred_element_type=F32)


def _sigmoid(x):
    return 1.0 / (1.0 + jnp.exp(-x))


def _gelu_tanh(x):
    c = 0.7978845608028654
    hx = 0.5 * x
    return hx + hx * jnp.tanh(x * (c + (c * 0.044715) * (x * x)))


def _layer_norm(x, g, b, eps=LN_EPS):
    mu = jnp.mean(x, axis=-1, keepdims=True)
    xc = x - mu
    var = jnp.mean(xc * xc, axis=-1, keepdims=True)
    return xc * lax.rsqrt(var + eps) * g + b


def _deepnorm(x, y_over_alpha, g, b):
    return _layer_norm(x + y_over_alpha, g, b, eps=LN_EPS / ALPHA ** 2)


def _layer_norm_one_pass(x, g, b):
    inv_n = 1.0 / x.shape[-1]
    mu = jnp.sum(x, axis=-1, keepdims=True) * inv_n
    var = jnp.sum(x * x, axis=-1, keepdims=True) * inv_n - mu * mu
    return (x - mu) * lax.rsqrt(var + LN_EPS) * g + b


def _resident(shape, layer=None):
    nd = len(shape)
    if layer is None:
        return pl.BlockSpec(shape, lambda *_: (0,) * nd, pipeline_mode=pl.Buffered(1))
    return pl.BlockSpec((None,) + tuple(shape), lambda *_: (layer,) + (0,) * nd,
                        pipeline_mode=pl.Buffered(1))


class _SideCast(NamedTuple):
    w: jax.Array
    layer: int | None = None
    scale: float = 1.0


BF16_SUBLANES = 16


def _side_cast_specs(casts, steps, step_of):
    in_specs, out_specs, out_shapes = [], [], []
    for c in casts:
        k, n = c.w.shape[-2:]
        cuts = [(k // r, n // cc) for r in range(BF16_SUBLANES, k + 1, BF16_SUBLANES) if k % r == 0
                for cc in range(LANES, n + 1, LANES) if n % cc == 0
                if steps % ((k // r) * (n // cc)) == 0]
        nr, ncol = max(cuts, key=lambda rc: (rc[0] * rc[1], -rc[1]))
        per = steps // (nr * ncol)

        def cell(*g, per=per, ncol=ncol):
            i = step_of(*g) // per
            return i // ncol, i % ncol

        block = (k // nr, n // ncol)
        if c.layer is None:
            in_specs.append(pl.BlockSpec(block, cell))
        else:
            in_specs.append(pl.BlockSpec((None,) + block, lambda *g, cell=cell, l=c.layer: (l, *cell(*g))))
        out_specs.append(pl.BlockSpec(block, cell))
        out_shapes.append(jax.ShapeDtypeStruct((k, n), BF16))
    return in_specs, out_specs, out_shapes


def _with_side_casts(body, n_in, n_out, casts):
    nc = len(casts)

    def kernel(*refs):
        cast_in = refs[n_in:n_in + nc]
        cast_out = refs[n_in + nc + n_out:n_in + 2 * nc + n_out]
        for c, src, dst in zip(casts, cast_in, cast_out):
            w = src[...]
            dst[...] = (w if c.scale == 1.0 else w * c.scale).astype(BF16)
        body(*refs[:n_in], *refs[n_in + nc:n_in + nc + n_out], *refs[n_in + 2 * nc + n_out:])

    return kernel


def _params(semantics):
    return pltpu.CompilerParams(dimension_semantics=semantics,
                                vmem_limit_bytes=VMEM_LIMIT_BYTES)


def _gmlp_kernel(x_ref, wi_ref, lng_ref, lnb_ref, ws_ref, bs_ref, wo_ref,
                 g1_ref, b1_ref, o_ref, vn_ref, y_ref, *, tb, half, layer):
    gdim = half // GM_GROUPS
    x = x_ref[...]
    xb = x.astype(BF16)
    pair = 2 * gdim
    u_pair = lambda j: _gelu_tanh(_dot(xb, wi_ref[:, j * pair:(j + 1) * pair]))
    v = _gelu_tanh(_dot(xb, wi_ref[:, half:]))
    u_next = u_pair(0)
    vn_ref[...] = _layer_norm_one_pass(v, lng_ref[...], lnb_ref[...]).astype(BF16)

    row = lax.broadcasted_iota(jnp.int32, (CHUNK, CHUNK), 0)
    col = lax.broadcasted_iota(jnp.int32, (CHUNK, CHUNK), 1)
    causal = col <= row
    bs = bs_ref[...]
    for j in range(GM_GROUPS // 2):
        u2 = u_next if j == 0 else u_pair(j)
        for gg in range(2):
            g = 2 * j + gg
            wsg = jnp.where(causal, ws_ref[g], 0.0).astype(BF16)
            bg = bs[:, g:g + 1]
            for c in range(tb // CHUNK):
                rows = slice(c * CHUNK, (c + 1) * CHUNK)
                sv = _dot(wsg, vn_ref[rows, g * gdim:(g + 1) * gdim]) + bg
                y_ref[rows, g * gdim:(g + 1) * gdim] = (
                    u2[rows, gg * gdim:(gg + 1) * gdim] * sv).astype(BF16)
    for r in range(2):
        rows = slice(r * (tb // 2), (r + 1) * (tb // 2))
        y = _dot(y_ref[rows, :], wo_ref[...])
        o_ref[rows, :] = _deepnorm(x_ref[rows, :], y, g1_ref[layer:layer + 1, :],
                                   b1_ref[layer:layer + 1, :])


def _gmlp_layer(x2d, w_in, ln_g, ln_b, w_s, b_s, w_out, g1, b1, *, layer, tb, casts=()):
    t, d = x2d.shape
    half = w_in.shape[1] // 2
    assert t % tb == 0 and tb % CHUNK == 0 and half % (GM_GROUPS * LANES) == 0
    row = lambda a: a.reshape(1, -1)
    tok = pl.BlockSpec((tb, d), lambda i: (i, 0))
    c_in, c_out, c_shapes = _side_cast_specs(casts, t // tb, lambda i: i)
    in_specs = [tok, _resident((d, 2 * half)),
                _resident((1, half)), _resident((1, half)),
                _resident(w_s.shape), _resident((CHUNK, GM_GROUPS)),
                _resident((half, d)), _resident(g1.shape), _resident(b1.shape)]
    return pl.pallas_call(
        _with_side_casts(functools.partial(_gmlp_kernel, tb=tb, half=half, layer=layer), len(in_specs), 1, casts),
        out_shape=(jax.ShapeDtypeStruct((t, d), F32), *c_shapes),
        grid=(t // tb,),
        in_specs=in_specs + c_in,
        out_specs=(tok, *c_out),
        scratch_shapes=[pltpu.VMEM((tb, half), BF16), pltpu.VMEM((tb, half), BF16)],
        compiler_params=_params(("arbitrary",)),
        name="gmlp_layer",
    )(x2d, w_in.astype(BF16), row(ln_g), row(ln_b), w_s, b_s.T,
      (w_out * (1.0 / ALPHA)).astype(BF16), g1, b1, *[c.w for c in casts])


def _ffn_kernel(x_ref, p_ref, wg_ref, wu_ref, wd_ref, g2_ref, b2_ref,
                wpg_ref, bpg_ref, wpp_ref, o_ref, *, layer):
    x = x_ref[...]
    xb = x.astype(BF16)
    dff = wg_ref.shape[1]
    cut = (dff // MXU_TILE // 2) * MXU_TILE
    y = None
    for cols in (slice(0, cut), slice(cut, dff)):
        g = _dot(xb, wg_ref[:, cols])
        u = _dot(xb, wu_ref[:, cols])
        h = (g * _sigmoid(g) * u).astype(BF16)
        part = _dot(h, wd_ref[cols, :])
        y = part if y is None else y + part
    pp = _dot(p_ref[...].astype(BF16), wpp_ref[...])
    vec = slice(layer, layer + 1)
    x2 = _deepnorm(x, y, g2_ref[vec, :], b2_ref[vec, :])
    gate = _sigmoid(_dot(x2.astype(BF16), wpg_ref[...]) + bpg_ref[vec, :])
    o_ref[...] = x2 + gate * pp


def _ffn_layer(x2d, p3d, w_gate, w_up, w_down, g2, b2, wpg, bpg, wpp, *, layer, tb, casts=()):
    t, d = x2d.shape
    dff = w_gate.shape[-1]
    dp = p3d.shape[-1]
    assert t % tb == 0
    tok = pl.BlockSpec((tb, d), lambda i: (i, 0))
    c_in, c_out, c_shapes = _side_cast_specs(casts, t // tb, lambda i: i)
    in_specs = [tok, pl.BlockSpec((None, tb, dp), lambda i: (layer, i, 0)),
                _resident((d, dff)), _resident((d, dff)), _resident((dff, d)),
                _resident(g2.shape), _resident(b2.shape),
                _resident((d, d)), _resident(bpg.shape), _resident((dp, d))]
    return pl.pallas_call(
        _with_side_casts(functools.partial(_ffn_kernel, layer=layer), len(in_specs), 1, casts),
        out_shape=(jax.ShapeDtypeStruct((t, d), F32), *c_shapes),
        grid=(t // tb,),
        in_specs=in_specs + c_in,
        out_specs=(tok, *c_out),
        compiler_params=_params(("arbitrary",)),
        name="ffn_layer",
    )(x2d, p3d, w_gate, w_up, w_down, g2, b2, wpg, bpg, wpp, *[c.w for c in casts])


HALO = 8


def _dot_nt(a, b):
    return lax.dot_general(a, b, (((1,), (1,)), ((), ())), preferred_element_type=F32)


def _mlstm_proj_kernel(x_ref, wi_ref, cw_ref, cb_ref, wqk_ref, wkt_ref, wv_ref, wg_ref, bg_ref,
                       skip_ref, ng_ref, q_ref, kt_ref, v_ref, cz_ref, gz_ref, gt_ref, ext_ref,
                       fc_ref, fm_ref, *, tb, inner):
    ntile = inner // MXU_TILE

    @pl.when((pl.program_id(0) == 0) & (pl.program_id(1) == 0))
    def _():
        for g in range(ntile):
            sl = slice(g * MXU_TILE, (g + 1) * MXU_TILE)
            wg_q = wg_ref[g * MXU_TILE:(g + 1) * MXU_TILE, :]
            wg_k = wg_ref[inner + g * MXU_TILE:inner + (g + 1) * MXU_TILE, :]
            wg_v = wg_ref[2 * inner + g * MXU_TILE:2 * inner + (g + 1) * MXU_TILE, :]
            fc_ref[sl, :] = (_dot(wqk_ref[g, :, :MXU_TILE], wg_q)
                             + _dot(wqk_ref[g, :, MXU_TILE:], wg_k)).astype(BF16)
            fm_ref[sl, :] = _dot(wv_ref[g], wg_v).astype(BF16)

    @pl.when(pl.program_id(1) == 0)
    def _():
        ext_ref[0:HALO, :] = jnp.zeros((HALO, inner), F32)

    xb = x_ref[...].astype(BF16)
    xm = _dot(xb, wi_ref[:, :inner])
    ext_ref[HALO:HALO + tb, :] = xm
    z = _dot(xb, wi_ref[:, inner:])
    sz = z * _sigmoid(z)
    gz_ref[...] = (ng_ref[...] * sz).astype(BF16)
    cw = cw_ref[...]
    conv = cw[ML_CONV - 1:ML_CONV, :] * xm + cb_ref[...]
    for j in range(1, ML_CONV):
        conv = conv + cw[ML_CONV - 1 - j:ML_CONV - j, :] * ext_ref[HALO - j:HALO - j + tb, :]
    ext_ref[0:HALO, :] = ext_ref[tb:tb + HALO, :]
    xc = conv * _sigmoid(conv)
    xcb = xc.astype(BF16)
    xmb = xm.astype(BF16)
    cz_ref[...] = (skip_ref[...] * xc * sz).astype(BF16)

    gates = jnp.broadcast_to(bg_ref[...], (tb, LANES))
    for g in range(ntile):
        sl = slice(g * MXU_TILE, (g + 1) * MXU_TILE)
        q_ref[:, sl] = _dot(xcb[:, sl], wqk_ref[g, :, :MXU_TILE]).astype(BF16)
        kt_ref[sl, :] = _dot_nt(wkt_ref[g], xcb[:, sl]).astype(BF16)
        v_ref[:, sl] = _dot(xmb[:, sl], wv_ref[g]).astype(BF16)
        gates = gates + _dot(xcb[:, sl], fc_ref[sl, :]) + _dot(xmb[:, sl], fm_ref[sl, :])
    gt_ref[...] = gates


def _block_diag_tiles(w, tile):
    nb, c, d = w.shape
    assert c == d and tile % c == 0 and nb % (tile // c) == 0
    cols = lax.broadcasted_iota(jnp.int32, (d, tile), 1)
    expand = (cols % d == lax.broadcasted_iota(jnp.int32, (d, tile), 0)).astype(F32)
    tiled = jnp.dot(w.reshape(nb * c, d), expand, precision=lax.Precision.HIGHEST)
    r = lax.broadcasted_iota(jnp.int32, (tile, tile), 0) // c
    s = lax.broadcasted_iota(jnp.int32, (tile, tile), 1) // d
    return jnp.where(r == s, tiled.reshape(nb * c // tile, tile, tile), 0.0).astype(BF16)


def _mlstm_proj(x2d, w_in, conv_w, conv_b, w_q, w_k, w_v, w_gates, b_gates, skip, norm_g,
                *, batch, tb):
    t, d = x2d.shape
    inner = w_in.shape[1] // 2
    seq = t // batch
    assert seq % tb == 0 and inner % MXU_TILE == 0 and tb >= HALO
    nblk = seq // tb
    ntile = inner // MXU_TILE
    wqk = jnp.concatenate([_block_diag_tiles(w_q, MXU_TILE), _block_diag_tiles(w_k, MXU_TILE)],
                          axis=-1)
    wkt = _block_diag_tiles(jnp.swapaxes(w_k, 1, 2), MXU_TILE)
    wv = _block_diag_tiles(w_v, MXU_TILE)
    ng = w_gates.shape[1]
    wg = jnp.pad(w_gates, ((0, 0), (0, LANES - ng))).astype(BF16)
    bg = jnp.pad(b_gates, (0, LANES - ng)).reshape(1, LANES)
    tok = lambda w: pl.BlockSpec((tb, w), lambda b, j: (b * nblk + j, 0))
    tok_t = pl.BlockSpec((None, inner, tb), lambda b, j: (b, 0, j))
    act = jax.ShapeDtypeStruct((t, inner), BF16)
    act_t = jax.ShapeDtypeStruct((batch, inner, seq), BF16)
    return pl.pallas_call(
        functools.partial(_mlstm_proj_kernel, tb=tb, inner=inner),
        out_shape=(act, act_t, act, act, act, jax.ShapeDtypeStruct((t, LANES), F32)),
        grid=(batch, nblk),
        in_specs=[tok(d), _resident((d, 2 * inner)),
                  _resident((ML_CONV, inner)), _resident((1, inner)),
                  _resident((ntile, MXU_TILE, 2 * MXU_TILE)), _resident((ntile, MXU_TILE, MXU_TILE)),
                  _resident((ntile, MXU_TILE, MXU_TILE)),
                  _resident((3 * inner, LANES)), _resident((1, LANES)),
                  _resident((1, inner)), _resident((1, inner))],
        out_specs=(tok(inner), tok_t, tok(inner), tok(inner), tok(inner), tok(LANES)),
        scratch_shapes=[pltpu.VMEM((tb + HALO, inner), F32),
                        pltpu.VMEM((inner, LANES), BF16),
                        pltpu.VMEM((inner, LANES), BF16)],
        compiler_params=_params(("arbitrary", "arbitrary")),
        name="mlstm_proj",
    )(x2d, w_in, conv_w, conv_b.reshape(1, inner), wqk, wkt, wv, wg, bg,
      skip.reshape(1, inner), norm_g.reshape(1, inner))


def _split3(a):
    hi = a.astype(BF16)
    r1 = a - hi.astype(F32)
    mid = r1.astype(BF16)
    lo = (r1 - mid.astype(F32)).astype(BF16)
    return hi, mid, lo


CELL_CHUNK = 256

def _mlstm_cell_kernel(q_ref, kt_ref, v_ref, cz_ref, gz_ref, gt_ref, x_ref,
                       wo_ref, g1_ref, b1_ref, o_ref,
                       c_ref, n_ref, m_ref, hg_ref, cb_ref, *, nb, tb, dh, lc, layer):
    nh = ML_HEADS
    scale = dh ** -0.5
    log_scale = -0.5 * math.log(dh)

    @pl.when(pl.program_id(1) == 0)
    def _():
        for ref in (c_ref, n_ref, m_ref):
            ref[...] = jnp.zeros(ref.shape, F32)

    row = lax.broadcasted_iota(jnp.int32, (lc, lc), 0)
    col = lax.broadcasted_iota(jnp.int32, (lc, lc), 1)
    causal = col <= row
    tri = jnp.where(causal, 1.0, 0.0).astype(BF16)
    lane = lax.broadcasted_iota(jnp.int32, (lc, LANES), 1)

    for c in range(tb // lc):
        rows = slice(c * lc, (c + 1) * lc)
        for b in range(nb):
            gt = gt_ref[b, rows, :]
            lf = jnp.minimum(gt, 0.0) - jnp.log(1.0 + jnp.exp(-jnp.abs(gt)))
            hi, mid, lo = _split3(lf)
            bcum = _dot(tri, hi) + _dot(tri, mid) + _dot(tri, lo)
            colmat = jnp.where(lane < nh, gt, bcum)
            rowmat = colmat.T

            def early(h):
                hs = slice(h * dh, (h + 1) * dh)
                hd = dict(st=b * nh + h, hs=hs,
                          i_row=rowmat[h:h + 1, :], b_row=rowmat[nh + h:nh + h + 1, :],
                          b_col=colmat[:, nh + h:nh + h + 1],
                          q=q_ref[b, rows, hs], kt=kt_ref[b, hs, rows], v=v_ref[b, rows, hs])
                st = hd["st"]
                hd["m_prev"] = m_ref[st:st + 1, 0:1]
                hd["n_prev"] = n_ref[st]
                cb_ref[st] = c_ref[st].astype(BF16)
                hd["qk"] = _dot(hd["q"], hd["kt"])
                return hd

            heads = [early(h) for h in range(nh)]

            def update(hd):
                b_last = hd["b_row"][:, lc - 1:lc]
                g_row = b_last - hd["b_row"] + hd["i_row"]
                m_new = jnp.maximum(b_last + hd["m_prev"], jnp.max(g_row, axis=-1, keepdims=True))
                wg_row = jnp.exp(g_row - m_new)
                hd["decay"] = jnp.exp(b_last + hd["m_prev"] - m_new)
                hd["m_new"] = m_new
                kw = (hd["kt"].astype(F32) * wg_row).astype(BF16)
                hd["c_upd"] = _dot(kw, hd["v"])
                hd["n_upd"] = _dot_nt(jnp.ones((8, lc), BF16), kw)

            def output(hd):
                st, hs = hd["st"], hd["hs"]
                dmat = jnp.where(causal, hd["b_col"] - hd["b_row"] + hd["i_row"], -jnp.inf)
                m_inter = hd["b_col"] + hd["m_prev"]
                m_t = jnp.maximum(m_inter, jnp.max(dmat, axis=-1, keepdims=True))
                s = hd["qk"] * jnp.exp(dmat - (m_t - log_scale))
                si = jnp.exp(m_inter - m_t) * scale
                qs = hd["q"].astype(F32) * si
                num = _dot(s.astype(BF16), hd["v"]) + _dot(qs.astype(BF16), cb_ref[st])
                den = (jnp.sum(s, axis=-1, keepdims=True)
                       + jnp.sum(qs * hd["n_prev"], axis=-1, keepdims=True))
                r = 1.0 / jnp.maximum(jnp.abs(den), jnp.exp(-m_t))
                nc = num - jnp.mean(num, axis=-1, keepdims=True)
                var = jnp.mean(nc * nc, axis=-1, keepdims=True)
                hn = nc * (r * lax.rsqrt(r * r * var + LN_EPS))
                hg = hn * gz_ref[b, rows, hs].astype(F32) + cz_ref[b, rows, hs].astype(F32)
                hg_ref[b, rows, hs] = hg.astype(BF16)

            def advance(hd):
                st = hd["st"]
                c_ref[st] = hd["decay"] * c_ref[st] + hd["c_upd"]
                n_ref[st] = hd["decay"] * hd["n_prev"] + hd["n_upd"][0:1, :]
                m_ref[st:st + 1, :] = jnp.broadcast_to(hd["m_new"], (1, LANES))

            for hd in heads:
                update(hd)
            for hd in heads:
                advance(hd)
            for hd in heads:
                output(hd)

            y = _dot(hg_ref[b, rows, :], wo_ref[...])
            o_ref[b, rows, :] = _deepnorm(x_ref[b, rows, :], y, g1_ref[layer:layer + 1, :],
                                          b1_ref[layer:layer + 1, :])


def _mlstm_cell_layer(q, kt, v, cz, gz, gt, x3d, w_out, g1, b1, *, layer, nb, tb, casts=()):
    batch, seq, inner = q.shape
    d = x3d.shape[-1]
    dh = inner // ML_HEADS
    assert seq % tb == 0 and tb % CELL_CHUNK == 0 and batch % nb == 0
    tok = lambda w: pl.BlockSpec((nb, tb, w), lambda bi, j: (bi, j, 0))
    tok_t = pl.BlockSpec((nb, inner, tb), lambda bi, j: (bi, 0, j))
    nstate = nb * ML_HEADS
    nblk = seq // tb
    c_in, c_out, c_shapes = _side_cast_specs(casts, (batch // nb) * nblk, lambda bi, j: bi * nblk + j)
    in_specs = [tok(inner), tok_t, tok(inner), tok(inner), tok(inner), tok(LANES), tok(d),
                _resident((inner, d)), _resident(g1.shape), _resident(b1.shape)]
    return pl.pallas_call(
        _with_side_casts(functools.partial(_mlstm_cell_kernel, nb=nb, tb=tb, dh=dh, lc=CELL_CHUNK, layer=layer),
                         len(in_specs), 1, casts),
        out_shape=(jax.ShapeDtypeStruct((batch, seq, d), F32), *c_shapes),
        grid=(batch // nb, nblk),
        in_specs=in_specs + c_in,
        out_specs=(tok(d), *c_out),
        scratch_shapes=[pltpu.VMEM((nstate, dh, dh), F32),
                        pltpu.VMEM((nstate, 1, dh), F32),
                        pltpu.VMEM((-(-nstate // 8) * 8, LANES), F32),
                        pltpu.VMEM((nb, tb, inner), BF16),
                        pltpu.VMEM((nstate, dh, dh), BF16)],
        compiler_params=_params(("arbitrary", "arbitrary")),
        name="mlstm_cell",
    )(q, kt, v, cz, gz, gt, x3d, w_out, g1, b1, *[c.w for c in casts])


def kernel(x, p, gm_w_in, gm_ln_g, gm_ln_b, gm_w_s, gm_b_s, gm_w_out, ml_w_in, ml_conv_w, ml_conv_b, ml_w_q, ml_w_k, ml_w_v, ml_w_gates, ml_b_gates, ml_skip, ml_norm_g, ml_w_out, ln1_g, ln1_b, ln2_g, ln2_b, ffn_w_gate, ffn_w_up, ffn_w_down, ple_w_proj, ple_w_gate, ple_b_gate):
    batch, seq, d = x.shape
    t = batch * seq
    xs = x.reshape(t, d)
    ps = p.reshape(p.shape[0], t, p.shape[-1])
    ln1g, ln1b, ln2g, ln2b, pbg = ln1_g, ln1_b, ln2_g, ln2_b, ple_b_gate

    def ffn_casts(i):
        return (_SideCast(ffn_w_gate, i), _SideCast(ffn_w_up, i),
                _SideCast(ffn_w_down, i, 1.0 / ALPHA), _SideCast(ple_w_gate, i),
                _SideCast(ple_w_proj, i))

    def ffn(xs, i, weights, casts=()):
        return _ffn_layer(xs, ps, *weights[:3], ln2g, ln2b, weights[3], pbg, weights[4],
                          layer=i, tb=512, casts=casts)

    xs, *ffn0_w = _gmlp_layer(xs, gm_w_in[0], gm_ln_g, gm_ln_b, gm_w_s[0], gm_b_s[0],
                              gm_w_out[0], ln1g, ln1b, layer=0, tb=512, casts=ffn_casts(0))
    xs, ml_w_in_b, ml_w_out_b, *ffn1_w = ffn(
        xs, 0, ffn0_w,
        casts=(_SideCast(ml_w_in, 0), _SideCast(ml_w_out, 0, 1.0 / ALPHA)) + ffn_casts(1))
    q, kt, v, cz, gz, gt = _mlstm_proj(xs, ml_w_in_b, ml_conv_w[0], ml_conv_b, ml_w_q[0],
                                       ml_w_k[0], ml_w_v[0], ml_w_gates[0], ml_b_gates[0],
                                       ml_skip, ml_norm_g[0], batch=batch, tb=512)
    seq3 = lambda a: a.reshape(batch, seq, a.shape[-1])
    xs, = _mlstm_cell_layer(seq3(q), kt, seq3(v), seq3(cz), seq3(gz), seq3(gt), seq3(xs),
                            ml_w_out_b, ln1g, ln1b, layer=1, nb=1, tb=256)
    xs, = ffn(xs.reshape(t, d), 1, ffn1_w)
    return xs.reshape(batch, seq, d)
```
